```python
import math
import jax
import jax.numpy as jnp
from jax import lax
import numpy as np

D_MODEL = 1024
BATCH = 2
SEQ = 8192
DEPTH = 2
DEC_BATCH = 32
DEC_SEQ = 8
PAST_LEN = 8192
PAGE_SIZE = 128

MIX_WIDTH = D_MODEL
S5_WIDTH = MIX_WIDTH // 2
S5_GROUP_CH = 16
S5_GROUPS = S5_WIDTH // S5_GROUP_CH
S5_STATE = 64
HGRN_DK = 128
HGRN_HEADS = (MIX_WIDTH // 2) // HGRN_DK
HGRN_DV = HGRN_DK
HGRN_CHUNK = 32
ATT_HEAD_DIM = 64
FOX_HEADS = (MIX_WIDTH // 2) // ATT_HEAD_DIM
SB_HEADS = FOX_HEADS
Q_BLOCK = 128
D_FF = 2816
N_ADA = 9
EPS = 1e-6
FOX_BIAS_INIT = 6.0
SB_BIAS_INIT = -5.0
AB_IN = S5_WIDTH + 4 * HGRN_HEADS * HGRN_DK
CD_IN = 3 * FOX_HEADS * ATT_HEAD_DIM + FOX_HEADS + 3 * SB_HEADS * ATT_HEAD_DIM

kernel_name = 'hybrid_s5_hgrn2_fox_stickbreak_step'


def rms_norm(x):
    xf = x.astype(jnp.float32)
    return (xf * lax.rsqrt(jnp.mean(xf * xf, axis=-1, keepdims=True) + EPS)).astype(x.dtype)


def modulate(x, shift, scale):
    return rms_norm(x) * (1.0 + scale) + shift


def swiglu(h, w_in, w_out):
    g, u = jnp.split(h @ w_in, 2, axis=-1)
    return (jax.nn.silu(g) * u) @ w_out


def cmul(ar, ai, br, bi):
    return ar * br - ai * bi, ar * bi + ai * br


def s5_mix(u, h0_re, h0_im, a_re, a_im, log_dt, b_re, b_im, c_re, c_im, d_skip, glu_w, glu_b):
    f32 = jnp.float32
    nb, L, _ = u.shape
    uf = u.astype(f32).reshape(nb, L, S5_GROUPS, S5_GROUP_CH)
    lam_re = a_re.astype(f32)
    lam_im = a_im.astype(f32)
    dt = jnp.exp(log_dt.astype(f32))[:, None]
    mag = jnp.exp(lam_re * dt)
    lbar_re = mag * jnp.cos(lam_im * dt)
    lbar_im = mag * jnp.sin(lam_im * dt)
    den = lam_re * lam_re + lam_im * lam_im
    nr = lbar_re - 1.0
    zr = (nr * lam_re + lbar_im * lam_im) / den
    zi = (lbar_im * lam_re - nr * lam_im) / den
    bb_re, bb_im = cmul(zr[..., None], zi[..., None], b_re.astype(f32), b_im.astype(f32))
    bu_re = jnp.einsum('blgh,gph->blgp', uf, bb_re)
    bu_im = jnp.einsum('blgh,gph->blgp', uf, bb_im)
    ar = jnp.broadcast_to(lbar_re, bu_re.shape)
    ai = jnp.broadcast_to(lbar_im, bu_im.shape)

    def combine(e1, e2):
        a1r, a1i, b1r, b1i = e1
        a2r, a2i, b2r, b2i = e2
        nar, nai = cmul(a2r, a2i, a1r, a1i)
        nbr, nbi = cmul(a2r, a2i, b1r, b1i)
        return nar, nai, nbr + b2r, nbi + b2i

    pr, pim, hr, hi = lax.associative_scan(combine, (ar, ai, bu_re, bu_im), axis=1)
    cr, ci = cmul(pr, pim, h0_re.astype(f32)[:, None], h0_im.astype(f32)[:, None])
    hr = hr + cr
    hi = hi + ci
    y = (jnp.einsum('blgp,ghp->blgh', hr, c_re.astype(f32))
         - jnp.einsum('blgp,ghp->blgh', hi, c_im.astype(f32))
         + d_skip.astype(f32) * uf)
    y = jax.nn.gelu(y.reshape(nb, L, S5_WIDTH))
    out = y * jax.nn.sigmoid(y @ glu_w.astype(f32) + glu_b.astype(f32))
    return out, hr[:, -1], hi[:, -1]


def hgrn2_scan(q, k, v, logf, S0):
    f32 = jnp.float32
    nb, L, nh, _ = q.shape
    dv = v.shape[-1]
    C = math.gcd(L, HGRN_CHUNK)
    n = L // C

    def chunks(t):
        return t.astype(f32).reshape(nb, n, C, nh, t.shape[-1]).transpose(1, 0, 3, 2, 4)

    causal = jnp.tril(jnp.ones((C, C), dtype=bool))

    def step(S, xs):
        qc, kc, vc, gc = xs
        b = jnp.cumsum(gc, axis=2)
        q_in = qc * jnp.exp(b)
        k_in = kc * jnp.exp(-b)
        scores = jnp.where(causal, jnp.einsum('bhtd,bhsd->bhts', q_in, k_in), 0.0)
        o = jnp.einsum('bhtd,bhde->bhte', q_in, S) + jnp.einsum('bhts,bhse->bhte', scores, vc)
        b_last = b[:, :, -1:, :]
        S_new = (jnp.exp(b_last[:, :, 0, :])[..., None] * S
                 + jnp.einsum('bhsd,bhse->bhde', kc * jnp.exp(b_last - b), vc))
        return S_new, o

    S_fin, o = lax.scan(step, S0.astype(f32), (chunks(q), chunks(k), chunks(v), chunks(logf)))
    o = o.transpose(1, 0, 3, 2, 4).reshape(nb, L, nh, dv)
    return o, S_fin


def ab_mixer(h, h0_re, h0_im, S0, layer_idx, ab_w_in, ab_w_out, s5_a_re, s5_a_im, s5_log_dt,
             s5_b_re, s5_b_im, s5_c_re, s5_c_im, s5_d, s5_glu_w, s5_glu_b, hgrn_lb_logits, hgrn_norm_w):
    f32 = jnp.float32
    nb, L, _ = h.shape
    proj = h @ ab_w_in
    u = proj[..., :S5_WIDTH]
    hw = HGRN_HEADS * HGRN_DK
    q_raw, f_raw, i_raw, g_raw = [
        proj[..., S5_WIDTH + j * hw:S5_WIDTH + (j + 1) * hw].reshape(nb, L, HGRN_HEADS, HGRN_DK)
        for j in range(4)]
    s5_out, s5_re, s5_im = s5_mix(u, h0_re, h0_im, s5_a_re, s5_a_im, s5_log_dt, s5_b_re, s5_b_im,
                                  s5_c_re, s5_c_im, s5_d, s5_glu_w, s5_glu_b)
    lb = jnp.cumsum(jax.nn.softmax(hgrn_lb_logits.astype(f32), axis=0), axis=0)[layer_idx]
    lb = lb.reshape(HGRN_HEADS, HGRN_DK)
    f = lb + (1.0 - lb) * jax.nn.sigmoid(f_raw.astype(f32))
    q = jax.nn.silu(q_raw.astype(f32))
    o, S_fin = hgrn2_scan(q, 1.0 - f, i_raw.astype(f32), jnp.log(f), S0)
    o = rms_norm(o) * hgrn_norm_w.astype(f32) * jax.nn.silu(g_raw.astype(f32))
    mixed = jnp.concatenate([s5_out, o.reshape(nb, L, hw)], axis=-1).astype(h.dtype)
    return mixed @ ab_w_out, (s5_re, s5_im, S_fin)


def fox_attend(q, k, v, fq_cum, fkT, q_pos, k_pos):
    s = jnp.einsum('bqhd,bkhd->bhqk', q, k).astype(jnp.float32) * ATT_HEAD_DIM ** -0.5
    s = s + jnp.swapaxes(fq_cum, 1, 2)[..., None] - fkT
    s = jnp.where(k_pos[None, :] <= q_pos[:, None], s, -jnp.inf)
    p = jax.nn.softmax(s, axis=-1)
    return jnp.einsum('bhqk,bkhd->bqhd', p.astype(v.dtype), v)


def sb_attend(q, k, v, sb_bias, q_pos, k_pos):
    z = jnp.einsum('bqhd,bkhd->bhqk', q, k).astype(jnp.float32) * ATT_HEAD_DIM ** -0.5
    z = z + sb_bias.astype(jnp.float32)[None, :, None, None]
    mask = k_pos[None, :] < q_pos[:, None]
    log_keep = jnp.where(mask, jax.nn.log_sigmoid(-z), 0.0)
    later = lax.cumsum(log_keep, axis=3, reverse=True) - log_keep
    w = jnp.where(mask, jnp.exp(jax.nn.log_sigmoid(z) + later), 0.0)
    return jnp.einsum('bhqk,bkhd->bqhd', w.astype(v.dtype), v)


def cd_attention(fq, fk, fv, fq_cum, fk_cum, sq, sk, sv, sb_bias, q_offset):
    nb, L, _, _ = fq.shape
    T = math.gcd(L, Q_BLOCK)
    n = L // T
    k_pos = jnp.arange(fk.shape[1])
    q_pos = q_offset + jnp.arange(L).reshape(n, T)
    fkT = jnp.swapaxes(fk_cum, 1, 2)[:, :, None, :]

    def blocks(t):
        return jnp.moveaxis(t.reshape((nb, n, T) + t.shape[2:]), 1, 0)

    def one(xs):
        fqb, fcb, sqb, qp = xs
        return (fox_attend(fqb, fk, fv, fcb, fkT, qp, k_pos), sb_attend(sqb, sk, sv, sb_bias, qp, k_pos))

    fo, so = lax.map(one, (blocks(fq), blocks(fq_cum), blocks(sq), q_pos))
    fo = jnp.moveaxis(fo, 0, 1).reshape(nb, L, FOX_HEADS * ATT_HEAD_DIM)
    so = jnp.moveaxis(so, 0, 1).reshape(nb, L, SB_HEADS * ATT_HEAD_DIM)
    return fo, so


def cd_mixer(h, past_fk, past_fv, past_flogf, past_sk, past_sv, cd_w_in, cd_b_f, cd_b_sb, cd_w_out):
    f32 = jnp.float32
    nb, L, _ = h.shape
    proj = h @ cd_w_in
    w = FOX_HEADS * ATT_HEAD_DIM
    fq, fk, fv = [proj[..., j * w:(j + 1) * w].reshape(nb, L, FOX_HEADS, ATT_HEAD_DIM) for j in range(3)]
    f_logit = proj[..., 3 * w:3 * w + FOX_HEADS]
    off = 3 * w + FOX_HEADS
    ws = SB_HEADS * ATT_HEAD_DIM
    sq, sk, sv = [proj[..., off + j * ws:off + (j + 1) * ws].reshape(nb, L, SB_HEADS, ATT_HEAD_DIM)
                  for j in range(3)]
    logf = jax.nn.log_sigmoid((f_logit + cd_b_f).astype(f32))
    f_new = jnp.cumsum(logf, axis=1)
    pl = past_flogf.astype(f32)
    f_past = -(jnp.flip(jnp.cumsum(jnp.flip(pl, 1), axis=1), 1) - pl)
    fk_cum = jnp.concatenate([f_past, f_new], axis=1)
    fk_all = jnp.concatenate([past_fk, fk], axis=1)
    fv_all = jnp.concatenate([past_fv, fv], axis=1)
    sk_all = jnp.concatenate([past_sk, sk], axis=1)
    sv_all = jnp.concatenate([past_sv, sv], axis=1)
    fo, so = cd_attention(fq, fk_all, fv_all, f_new, fk_cum, sq, sk_all, sv_all, cd_b_sb, past_fk.shape[1])
    mixed = jnp.concatenate([fo, so], axis=-1).astype(h.dtype)
    return mixed @ cd_w_out, (fk, fv, logf, sk, sv)


def gather_pages(pool, page_table):
    rows = pool[page_table]
    return rows.reshape((page_table.shape[0], -1) + pool.shape[2:])


def trunk(x, c, ab_state, cd_past, ab_weights, cd_weights, ada_w, ada_b, ffn_w_in, ffn_w_out, final_norm_w):
    new_states = []
    for l in range(DEPTH):
        mod = (jax.nn.silu(c) @ ada_w[l] + ada_b[l]).reshape(c.shape[0], N_ADA, 1, D_MODEL)
        h = modulate(x, mod[:, 0], mod[:, 1])
        x = x + 0.5 * mod[:, 2] * swiglu(h, ffn_w_in[l, 0], ffn_w_out[l, 0])
        h = modulate(x, mod[:, 3], mod[:, 4])
        if l % 2 == 0:
            out, st = ab_mixer(h, ab_state[0], ab_state[1], ab_state[2], l, *ab_weights)
        else:
            out, st = cd_mixer(h, cd_past[0], cd_past[1], cd_past[2], cd_past[3], cd_past[4], *cd_weights)
        x = x + mod[:, 5] * out
        h = modulate(x, mod[:, 6], mod[:, 7])
        x = x + 0.5 * mod[:, 8] * swiglu(h, ffn_w_in[l, 1], ffn_w_out[l, 1])
        new_states.append(st)
    return rms_norm(x) * final_norm_w, new_states


def setup_inputs(seed: int = 0) -> dict:
    key = jax.random.key(seed)
    ks = jax.random.split(key, 36)
    f32 = jnp.float32
    n_pages = PAST_LEN // PAGE_SIZE
    n_used = DEC_BATCH * n_pages
    n_phys = n_used + (n_used + 3) // 4

    def nrm(k, shape, s=1.0):
        return s * jax.random.normal(k, shape, f32)

    page_table = jax.random.permutation(ks[10], n_phys)[:n_used].reshape(DEC_BATCH, n_pages).astype(jnp.int32)
    return {
        'x_prompt': nrm(ks[0], (BATCH, SEQ, D_MODEL)),
        'x_sample': nrm(ks[1], (DEC_BATCH, DEC_SEQ, D_MODEL)),
        'state_s5_re': nrm(ks[2], (DEC_BATCH, S5_GROUPS, S5_STATE), 0.1),
        'state_s5_im': nrm(ks[3], (DEC_BATCH, S5_GROUPS, S5_STATE), 0.1),
        'state_hgrn': nrm(ks[4], (DEC_BATCH, HGRN_HEADS, HGRN_DK, HGRN_DV), 0.5),
        'cache_fox_k': nrm(ks[5], (n_phys, PAGE_SIZE, FOX_HEADS, ATT_HEAD_DIM)),
        'cache_fox_v': nrm(ks[6], (n_phys, PAGE_SIZE, FOX_HEADS, ATT_HEAD_DIM)),
        'cache_fox_logf': jax.nn.log_sigmoid(FOX_BIAS_INIT + nrm(ks[7], (n_phys, PAGE_SIZE, FOX_HEADS))),
        'cache_sb_k': nrm(ks[8], (n_phys, PAGE_SIZE, SB_HEADS, ATT_HEAD_DIM)),
        'cache_sb_v': nrm(ks[9], (n_phys, PAGE_SIZE, SB_HEADS, ATT_HEAD_DIM)),
        'page_table': page_table,
        'c_prompt': nrm(ks[11], (BATCH, D_MODEL)),
        'c_sample': nrm(ks[12], (DEC_BATCH, D_MODEL)),
        'ada_w': nrm(ks[13], (DEPTH, D_MODEL, N_ADA * D_MODEL), 0.5 * D_MODEL ** -0.5),
        'ada_b': nrm(ks[14], (DEPTH, N_ADA * D_MODEL), 0.01),
        'ffn_w_in': nrm(ks[15], (DEPTH, 2, D_MODEL, 2 * D_FF), D_MODEL ** -0.5),
        'ffn_w_out': nrm(ks[16], (DEPTH, 2, D_FF, D_MODEL), D_FF ** -0.5),
        'ab_w_in': nrm(ks[17], (D_MODEL, AB_IN), D_MODEL ** -0.5),
        'ab_w_out': nrm(ks[18], (MIX_WIDTH, D_MODEL), MIX_WIDTH ** -0.5),
        's5_a_re': -0.5 + nrm(ks[19], (S5_GROUPS, S5_STATE), 0.01),
        's5_a_im': math.pi * jnp.arange(S5_STATE, dtype=f32)[None, :] + nrm(ks[20], (S5_GROUPS, S5_STATE), 0.01),
        's5_log_dt': jax.random.uniform(ks[21], (S5_GROUPS,), f32, math.log(1e-3), math.log(1e-1)),
        's5_b_re': nrm(ks[22], (S5_GROUPS, S5_STATE, S5_GROUP_CH), S5_GROUP_CH ** -0.5),
        's5_b_im': nrm(ks[23], (S5_GROUPS, S5_STATE, S5_GROUP_CH), S5_GROUP_CH ** -0.5),
        's5_c_re': nrm(ks[24], (S5_GROUPS, S5_GROUP_CH, S5_STATE), S5_STATE ** -0.5),
        's5_c_im': nrm(ks[25], (S5_GROUPS, S5_GROUP_CH, S5_STATE), S5_STATE ** -0.5),
        's5_d': nrm(ks[26], (S5_GROUPS, S5_GROUP_CH)),
        's5_glu_w': nrm(ks[27], (S5_WIDTH, S5_WIDTH), S5_WIDTH ** -0.5),
        's5_glu_b': nrm(ks[28], (S5_WIDTH,), 0.01),
        'hgrn_lb_logits': nrm(ks[29], (DEPTH + 1, HGRN_HEADS * HGRN_DK), 0.1),
        'hgrn_norm_w': 1.0 + nrm(ks[30], (HGRN_DV,), 0.01),
        'cd_w_in': nrm(ks[31], (D_MODEL, CD_IN), D_MODEL ** -0.5),
        'cd_b_f': FOX_BIAS_INIT + nrm(ks[32], (FOX_HEADS,), 0.1),
        'cd_b_sb': SB_BIAS_INIT + nrm(ks[35], (SB_HEADS,), 0.1),
        'cd_w_out': nrm(ks[33], (MIX_WIDTH, D_MODEL), MIX_WIDTH ** -0.5),
        'final_norm_w': 1.0 + nrm(ks[34], (D_MODEL,), 0.01),
    }


def reference(x_prompt, x_sample, state_s5_re, state_s5_im, state_hgrn, cache_fox_k, cache_fox_v,
              cache_fox_logf, cache_sb_k, cache_sb_v, page_table, c_prompt, c_sample, ada_w, ada_b,
              ffn_w_in, ffn_w_out, ab_w_in, ab_w_out, s5_a_re, s5_a_im, s5_log_dt, s5_b_re, s5_b_im,
              s5_c_re, s5_c_im, s5_d, s5_glu_w, s5_glu_b, hgrn_lb_logits, hgrn_norm_w, cd_w_in, cd_b_f,
              cd_b_sb, cd_w_out, final_norm_w):
    f32 = jnp.float32
    ab_weights = (ab_w_in, ab_w_out, s5_a_re, s5_a_im, s5_log_dt, s5_b_re, s5_b_im, s5_c_re, s5_c_im,
                  s5_d, s5_glu_w, s5_glu_b, hgrn_lb_logits, hgrn_norm_w)
    cd_weights = (cd_w_in, cd_b_f, cd_b_sb, cd_w_out)

    bp = x_prompt.shape[0]
    ab_state_p = (jnp.zeros((bp, S5_GROUPS, S5_STATE), f32), jnp.zeros((bp, S5_GROUPS, S5_STATE), f32),
                  jnp.zeros((bp, HGRN_HEADS, HGRN_DK, HGRN_DV), f32))
    cd_past_p = (jnp.zeros((bp, 0, FOX_HEADS, ATT_HEAD_DIM), x_prompt.dtype),
                 jnp.zeros((bp, 0, FOX_HEADS, ATT_HEAD_DIM), x_prompt.dtype),
                 jnp.zeros((bp, 0, FOX_HEADS), f32),
                 jnp.zeros((bp, 0, SB_HEADS, ATT_HEAD_DIM), x_prompt.dtype),
                 jnp.zeros((bp, 0, SB_HEADS, ATT_HEAD_DIM), x_prompt.dtype))
    y_prompt, states_p = trunk(x_prompt, c_prompt, ab_state_p, cd_past_p, ab_weights, cd_weights,
                               ada_w, ada_b, ffn_w_in, ffn_w_out, final_norm_w)

    ab_state_s = (state_s5_re, state_s5_im, state_hgrn)
    cd_past_s = (gather_pages(cache_fox_k, page_table), gather_pages(cache_fox_v, page_table),
                 gather_pages(cache_fox_logf, page_table), gather_pages(cache_sb_k, page_table),
                 gather_pages(cache_sb_v, page_table))
    y_sample, states_s = trunk(x_sample, c_sample, ab_state_s, cd_past_s, ab_weights, cd_weights,
                               ada_w, ada_b, ffn_w_in, ffn_w_out, final_norm_w)

    (s5r_p, s5i_p, hg_p), (fk_p, fv_p, flf_p, sk_p, sv_p) = states_p
    (s5r_s, s5i_s, hg_s), (fk_s, fv_s, flf_s, sk_s, sv_s) = states_s
    return (y_prompt, y_sample,
            s5r_p, s5i_p, hg_p, fk_p, fv_p, flf_p, sk_p, sv_p,
            s5r_s, s5i_s, hg_s, fk_s, fv_s, flf_s, sk_s, sv_s)
```

```python
import functools
import math

import jax
import jax.numpy as jnp
from jax import lax
from jax.experimental import pallas as pl
from jax.experimental.pallas import tpu as pltpu

F32 = jnp.float32
BF16 = jnp.bfloat16
EPS = 1e-6
N_ADA = 9
S5_GROUP_CH = 16
S5_CHUNK = 64
HGRN_CHUNK = 64
LANES = 128
VMEM_LIMIT = 56 * 1024 * 1024


def _cparams(*sem):
    return pltpu.CompilerParams(dimension_semantics=sem, vmem_limit_bytes=VMEM_LIMIT)


def _tile(n, pref):
    if n <= pref:
        return n
    t = pref - pref % 8
    while n % t:
        t -= 8
    return t


def _bdot(a, b):
    return jnp.dot(a.astype(BF16), b.astype(BF16), preferred_element_type=F32)


def _bdot_nt(a, b):
    return lax.dot_general(a.astype(BF16), b.astype(BF16), (((1,), (1,)), ((), ())),
                           preferred_element_type=F32)


def _bdot_tn(a, b):
    return lax.dot_general(a.astype(BF16), b.astype(BF16), (((0,), (0,)), ((), ())),
                           preferred_element_type=F32)


def _split2(x):
    hi = x.astype(BF16)
    lo = (x - hi.astype(F32)).astype(BF16)
    return hi, lo


def _split3(x):
    hi = x.astype(BF16)
    r = x - hi.astype(F32)
    mid = r.astype(BF16)
    lo = (r - mid.astype(F32)).astype(BF16)
    return hi, mid, lo


def _sigmoid(x):
    return 1.0 / (1.0 + jnp.exp(-x))


def _silu(x):
    return x * _sigmoid(x)


def _softplus(x):
    return jnp.maximum(x, 0.0) + jnp.log(1.0 + jnp.exp(-jnp.abs(x)))


def _rms(x):
    return x * lax.rsqrt(jnp.mean(x * x, axis=-1, keepdims=True) + EPS)


def _modulate(x, shift, scale):
    return _rms(x) * (1.0 + scale) + shift


def _mod_spec(r, j, tpb, d):
    return pl.BlockSpec((1, r, d), lambda i: (i // tpb, 0, j))


def _const_spec(shape):
    nd = len(shape)
    return pl.BlockSpec(shape, lambda *_: (0,) * nd, pipeline_mode=pl.Buffered(1))


def _ada_kernel(c_ref, w_ref, b_ref, o_ref):
    s = _silu(c_ref[...])
    w = w_ref[0]
    sh, sl = _split2(s)
    wh, wl = _split2(w)
    acc = jnp.dot(sh, wh, preferred_element_type=F32)
    acc += jnp.dot(sh, wl, preferred_element_type=F32)
    acc += jnp.dot(sl, wh, preferred_element_type=F32)
    o_ref[0] = acc + b_ref[0]


def _ada(c_all, ada_w, ada_b):
    depth, d, nout = ada_w.shape
    m = c_all.shape[0]
    tn = _tile(nout, 1152) if nout % LANES == 0 else nout
    return pl.pallas_call(
        _ada_kernel,
        grid=(depth, nout // tn),
        in_specs=[pl.BlockSpec((m, d), lambda l, j: (0, 0)),
                  pl.BlockSpec((1, d, tn), lambda l, j: (l, 0, j)),
                  pl.BlockSpec((1, 1, tn), lambda l, j: (l, 0, j))],
        out_specs=pl.BlockSpec((1, m, tn), lambda l, j: (l, 0, j)),
        out_shape=jax.ShapeDtypeStruct((depth, m, nout), F32),
        compiler_params=_cparams("arbitrary", "arbitrary"),
        name="ada",
    )(c_all, ada_w, ada_b.reshape(depth, 1, nout))


def _ffn_kernel(*refs, dff, final):
    if final:
        x_ref, sh_ref, sc_ref, gt_ref, win_ref, wout_ref, fw_ref, o_ref = refs
    else:
        x_ref, sh_ref, sc_ref, gt_ref, win_ref, wout_ref, o_ref = refs
    x = x_ref[...]
    h = _modulate(x, sh_ref[0], sc_ref[0]).astype(BF16)
    gu = jnp.dot(h, win_ref[...], preferred_element_type=F32)
    g = gu[:, :dff]
    u = gu[:, dff:]
    a = (_silu(g) * u).astype(BF16)
    out = jnp.dot(a, wout_ref[...], preferred_element_type=F32)
    y = x + 0.5 * gt_ref[0] * out
    if final:
        y = _rms(y) * fw_ref[...]
    o_ref[...] = y


def _ffn(x, modg, j0, tpb, tm, w_in, w_out, final_w=None):
    n, d = x.shape
    r = modg.shape[1]
    dff = w_out.shape[0]
    final = final_w is not None
    in_specs = [pl.BlockSpec((tm, d), lambda i: (i, 0)),
                _mod_spec(r, j0, tpb, d), _mod_spec(r, j0 + 1, tpb, d), _mod_spec(r, j0 + 2, tpb, d),
                _const_spec(w_in.shape), _const_spec(w_out.shape)]
    args = [x, modg, modg, modg, w_in, w_out]
    if final:
        in_specs.append(_const_spec((1, d)))
        args.append(final_w.reshape(1, d))
    return pl.pallas_call(
        functools.partial(_ffn_kernel, dff=dff, final=final),
        grid=(n // tm,),
        in_specs=in_specs,
        out_specs=pl.BlockSpec((tm, d), lambda i: (i, 0)),
        out_shape=jax.ShapeDtypeStruct((n, d), F32),
        compiler_params=_cparams("parallel"),
        name="ffn",
    )(*args)


def _proj_kernel(x_ref, sh_ref, sc_ref, w_ref, o_ref):
    h = _modulate(x_ref[...], sh_ref[0], sc_ref[0]).astype(BF16)
    o_ref[...] = jnp.dot(h, w_ref[...], preferred_element_type=F32)


def _proj(x, modg, j0, tpb, tm, w):
    n, d = x.shape
    r = modg.shape[1]
    nout = w.shape[1]
    return pl.pallas_call(
        _proj_kernel,
        grid=(n // tm,),
        in_specs=[pl.BlockSpec((tm, d), lambda i: (i, 0)),
                  _mod_spec(r, j0, tpb, d), _mod_spec(r, j0 + 1, tpb, d),
                  _const_spec(w.shape)],
        out_specs=pl.BlockSpec((tm, nout), lambda i: (i, 0)),
        out_shape=jax.ShapeDtypeStruct((n, nout), F32),
        compiler_params=_cparams("parallel"),
        name="ab_in",
    )(x, modg, modg, w)


def _cmul_rows(x, a1, a2):
    return x * a1 + pltpu.roll(x, x.shape[-1] // 2, 1) * a2


def _s5_kernel(u_ref, tm_ref, sm_ref, ym_ref, a1_ref, a2_ref, h0_ref, y_ref, hf_ref, *, nb, nc):
    u = u_ref[0].astype(BF16)
    e = jnp.dot(u, sm_ref[0], preferred_element_type=F32)
    a1 = a1_ref[0]
    a2 = a2_ref[0]
    h0 = h0_ref[0]
    if nc == 1:
        hprev = h0
        hf_ref[0] = _cmul_rows(h0, a1[0:1], a2[0:1]) + e
    else:
        row = lax.broadcasted_iota(jnp.int32, (nc, e.shape[1]), 0)
        prevs = []
        for b in range(nb):
            h0b = h0[b:b + 1]
            x = e[b * nc:(b + 1) * nc]
            x = x + jnp.where(row == 0, _cmul_rows(h0b, a1[0:1], a2[0:1]), 0.0)
            d, k = 1, 0
            while d < nc:
                sh = jnp.where(row >= d, pltpu.roll(x, d, 0), 0.0)
                x = x + _cmul_rows(sh, a1[k:k + 1], a2[k:k + 1])
                d, k = 2 * d, k + 1
            hf_ref[0, b:b + 1, :] = x[nc - 1:nc]
            prevs.append(jnp.where(row == 0, h0b, pltpu.roll(x, 1, 0)))
        hprev = jnp.concatenate(prevs, axis=0)
    y_ref[0] = (jnp.dot(u, tm_ref[0], preferred_element_type=F32)
                + _bdot(hprev, ym_ref[0]))


def _s5_params(a_re, a_im, log_dt, b_re, b_im, c_re, c_im, d_skip, t, nc):
    hp = lax.Precision.HIGHEST
    g, p = a_re.shape
    hc = b_re.shape[2]
    dt = jnp.exp(log_dt.astype(F32))[:, None]
    lam_re, lam_im = a_re.astype(F32), a_im.astype(F32)
    mag = jnp.exp(lam_re * dt)
    lbar_re = mag * jnp.cos(lam_im * dt)
    lbar_im = mag * jnp.sin(lam_im * dt)
    den = lam_re * lam_re + lam_im * lam_im
    nr = lbar_re - 1.0
    zr = (nr * lam_re + lbar_im * lam_im) / den
    zi = (lbar_im * lam_re - nr * lam_im) / den
    bb_re = zr[..., None] * b_re - zi[..., None] * b_im
    bb_im = zr[..., None] * b_im + zi[..., None] * b_re

    def powers(exps):
        ef = jnp.asarray(exps, F32)[:, None, None]
        m = jnp.exp(lam_re * dt * ef)
        return m * jnp.cos(lam_im * dt * ef), m * jnp.sin(lam_im * dt * ef)

    pr, pi = powers(list(range(t + 1)))
    pb_re = pr[:t, :, :, None] * bb_re - pi[:t, :, :, None] * bb_im
    pb_im = pr[:t, :, :, None] * bb_im + pi[:t, :, :, None] * bb_re
    kk = (jnp.einsum('gop,tgpi->gtio', c_re, pb_re, precision=hp)
          - jnp.einsum('gop,tgpi->gtio', c_im, pb_im, precision=hp))
    kk = kk.at[:, 0].add(d_skip[:, :, None] * jnp.eye(hc, dtype=F32))
    lag = jnp.arange(t)[None, :] - jnp.arange(t)[:, None]
    tm = jnp.where((lag >= 0)[None, :, :, None, None], kk[:, jnp.clip(lag, 0, t - 1)], 0.0)
    tm = tm.transpose(0, 1, 3, 2, 4).reshape(g, t * hc, t * hc)
    sm = jnp.concatenate([pb_re[::-1].transpose(1, 0, 3, 2), pb_im[::-1].transpose(1, 0, 3, 2)], axis=-1)
    sm = sm.reshape(g, t * hc, 2 * p)
    p1r, p1i = pr[1:], pi[1:]
    y_r = c_re[None] * p1r[:, :, None, :] - c_im[None] * p1i[:, :, None, :]
    y_i = -c_re[None] * p1i[:, :, None, :] - c_im[None] * p1r[:, :, None, :]
    ym = jnp.concatenate([y_r, y_i], axis=-1).transpose(1, 3, 0, 2).reshape(g, 2 * p, t * hc)
    nk = max(1, int(math.log2(nc))) if nc > 1 else 1
    ar, ai = powers([t * (2 ** k) for k in range(nk)])
    a1 = jnp.concatenate([ar, ar], axis=-1).transpose(1, 0, 2)
    a2 = jnp.concatenate([-ai, ai], axis=-1).transpose(1, 0, 2)
    return tm.astype(BF16), sm.astype(BF16), ym.astype(BF16), a1, a2


def _s5(u, h0_re, h0_im, s5w, nb, seq):
    a_re, a_im, log_dt, b_re, b_im, c_re, c_im, d_skip = s5w
    g, p = a_re.shape
    hc = b_re.shape[2]
    t = min(S5_CHUNK, seq)
    nc = seq // t
    m = nb * nc
    tm, sm, ym, a1, a2 = _s5_params(a_re, a_im, log_dt, b_re, b_im, c_re, c_im, d_skip, t, nc)
    nk = a1.shape[1]
    uf = u.reshape(nb, nc, t, g, hc).transpose(3, 0, 1, 2, 4).reshape(g, m, t * hc)
    h0 = jnp.concatenate([h0_re, h0_im], axis=-1).transpose(1, 0, 2)
    y, hf = pl.pallas_call(
        functools.partial(_s5_kernel, nb=nb, nc=nc),
        grid=(g,),
        in_specs=[pl.BlockSpec((1, m, t * hc), lambda i: (i, 0, 0)),
                  pl.BlockSpec((1, t * hc, t * hc), lambda i: (i, 0, 0)),
                  pl.BlockSpec((1, t * hc, 2 * p), lambda i: (i, 0, 0)),
                  pl.BlockSpec((1, 2 * p, t * hc), lambda i: (i, 0, 0)),
                  pl.BlockSpec((1, nk, 2 * p), lambda i: (i, 0, 0)),
                  pl.BlockSpec((1, nk, 2 * p), lambda i: (i, 0, 0)),
                  pl.BlockSpec((1, nb, 2 * p), lambda i: (i, 0, 0))],
        out_specs=[pl.BlockSpec((1, m, t * hc), lambda i: (i, 0, 0)),
                   pl.BlockSpec((1, nb, 2 * p), lambda i: (i, 0, 0))],
        out_shape=[jax.ShapeDtypeStruct((g, m, t * hc), F32),
                   jax.ShapeDtypeStruct((g, nb, 2 * p), F32)],
        compiler_params=_cparams("parallel"),
        name="s5",
    )(uf, tm, sm, ym, a1, a2, h0)
    y = y.reshape(g, nb, nc, t, hc).transpose(1, 2, 3, 0, 4).reshape(nb * seq, g * hc)
    hf = hf.transpose(1, 0, 2)
    return y, hf[..., :p], hf[..., p:]


def _hgrn_kernel(q_ref, f_ref, i_ref, g_ref, lb_ref, nw_ref, s0_ref, o_ref, sf_ref, st_scr, *, c, nch):
    li = pl.program_id(2)

    @pl.when(li == 0)
    def _():
        st_scr[...] = s0_ref[0, 0]

    lb = lb_ref[...]
    nw = nw_ref[...]
    row = lax.broadcasted_iota(jnp.int32, (c, c), 0)
    col = lax.broadcasted_iota(jnp.int32, (c, c), 1)
    causal = col <= row
    tril = causal.astype(BF16)
    mid = max(c // 2 - 1, 0)

    def chunk(ci, carry):
        sl = pl.ds(pl.multiple_of(ci * c, c), c)
        f = lb + (1.0 - lb) * _sigmoid(f_ref[sl, :])
        k = 1.0 - f
        q = _silu(q_ref[sl, :])
        v = i_ref[sl, :]
        gh, gl = _split2(jnp.log(f))
        b = (jnp.dot(tril, gh, preferred_element_type=F32)
             + jnp.dot(tril, gl, preferred_element_type=F32))
        r = b[mid:mid + 1]
        sc = jnp.where(causal, _bdot_nt(q * jnp.exp(b - r), k * jnp.exp(r - b)), 0.0)
        st = st_scr[...]
        o = _bdot_nt(q * jnp.exp(b), st) + _bdot(sc, v)
        bl = b[c - 1:c]
        st_scr[...] = st * jnp.exp(bl) + _bdot_tn(v, k * jnp.exp(bl - b))
        o_ref[sl, :] = _rms(o) * nw * _silu(g_ref[sl, :])
        return carry

    lax.fori_loop(0, nch, chunk, 0)

    @pl.when(li == pl.num_programs(2) - 1)
    def _():
        sf_ref[0, 0] = st_scr[...]


def _hgrn(proj, col0, lb, norm_w, s0, nb, seq):
    nh, dk, dv = s0.shape[1:]
    n = nb * seq
    c = min(HGRN_CHUNK, seq)
    tl = _tile(seq, 1024)
    nl = seq // tl
    cb = col0 // dk
    s0t = s0.astype(F32).transpose(0, 1, 3, 2)

    def colspec(j):
        return pl.BlockSpec((tl, dk), lambda b, h, l: (b * nl + l, cb + j * nh + h))

    o, sft = pl.pallas_call(
        functools.partial(_hgrn_kernel, c=c, nch=tl // c),
        grid=(nb, nh, nl),
        in_specs=[colspec(0), colspec(1), colspec(2), colspec(3),
                  pl.BlockSpec((1, dk), lambda b, h, l: (0, h)),
                  pl.BlockSpec((1, dv), lambda b, h, l: (0, 0)),
                  pl.BlockSpec((1, 1, dv, dk), lambda b, h, l: (b, h, 0, 0))],
        out_specs=[pl.BlockSpec((tl, dv), lambda b, h, l: (b * nl + l, h)),
                   pl.BlockSpec((1, 1, dv, dk), lambda b, h, l: (b, h, 0, 0))],
        out_shape=[jax.ShapeDtypeStruct((n, nh * dv), F32),
                   jax.ShapeDtypeStruct((nb, nh, dv, dk), F32)],
        scratch_shapes=[pltpu.VMEM((dv, dk), F32)],
        compiler_params=_cparams("parallel", "parallel", "arbitrary"),
        name="hgrn",
    )(proj, proj, proj, proj, lb, norm_w.reshape(1, dv), s0t)
    return o, sft.transpose(0, 1, 3, 2)


def _gelu_tanh(x):
    return x * (0.5 * (1.0 + jnp.tanh(math.sqrt(2.0 / math.pi) * (x + 0.044715 * (x * x * x)))))


def _about_kernel(x_ref, gt_ref, ys_ref, oh_ref, gw_ref, gb_ref, wo_ref, o_ref, *, w5):
    y = _gelu_tanh(ys_ref[...])
    s5 = y * _sigmoid(_bdot(y, gw_ref[...]) + gb_ref[...])
    out = _bdot(s5, wo_ref[:w5, :]) + _bdot(oh_ref[...], wo_ref[w5:, :])
    o_ref[...] = x_ref[...] + gt_ref[0] * out


def _ab_out(x, modg, j, tpb, tm, ys, oh, glu_w, glu_b, w_out):
    n, d = x.shape
    r = modg.shape[1]
    w5 = ys.shape[1]
    wh = oh.shape[1]
    return pl.pallas_call(
        functools.partial(_about_kernel, w5=w5),
        grid=(n // tm,),
        in_specs=[pl.BlockSpec((tm, d), lambda i: (i, 0)), _mod_spec(r, j, tpb, d),
                  pl.BlockSpec((tm, w5), lambda i: (i, 0)), pl.BlockSpec((tm, wh), lambda i: (i, 0)),
                  _const_spec(glu_w.shape), _const_spec((1, w5)), _const_spec(w_out.shape)],
        out_specs=pl.BlockSpec((tm, d), lambda i: (i, 0)),
        out_shape=jax.ShapeDtypeStruct((n, d), F32),
        compiler_params=_cparams("parallel"),
        name="ab_out",
    )(x, modg, ys, oh, glu_w, glu_b.reshape(1, w5), w_out)


def _cdout_kernel(x_ref, gt_ref, fo_ref, so_ref, wo_ref, o_ref, *, wf):
    out = _bdot(fo_ref[...], wo_ref[:wf, :]) + _bdot(so_ref[...], wo_ref[wf:, :])
    o_ref[...] = x_ref[...] + gt_ref[0] * out


def _cd_out(x, modg, j, tpb, tm, fo, so, w_out):
    n, d = x.shape
    r = modg.shape[1]
    wf = fo.shape[1]
    ws = so.shape[1]
    return pl.pallas_call(
        functools.partial(_cdout_kernel, wf=wf),
        grid=(n // tm,),
        in_specs=[pl.BlockSpec((tm, d), lambda i: (i, 0)), _mod_spec(r, j, tpb, d),
                  pl.BlockSpec((tm, wf), lambda i: (i, 0)), pl.BlockSpec((tm, ws), lambda i: (i, 0)),
                  _const_spec(w_out.shape)],
        out_specs=pl.BlockSpec((tm, d), lambda i: (i, 0)),
        out_shape=jax.ShapeDtypeStruct((n, d), F32),
        compiler_params=_cparams("parallel"),
        name="cd_out",
    )(x, modg, fo, so, w_out)


def _cdin_kernel(x_ref, sh_ref, sc_ref, wm_ref, wf_ref, bf_ref,
                 fqb_ref, fk_ref, fkb_ref, fv_ref, fvb_ref, sqb_ref, sk_ref, skb_ref, sv_ref, svb_ref,
                 lf_ref, fc_ref, carry_scr, *, w, nh, seg, tps, qscale):
    i = pl.program_id(0)
    tm = x_ref.shape[0]
    h = _modulate(x_ref[...], sh_ref[0], sc_ref[0]).astype(BF16)
    pm = jnp.dot(h, wm_ref[...], preferred_element_type=F32)
    fqb_ref[...] = (pm[:, 0:w] * qscale).astype(BF16)
    fk = pm[:, w:2 * w]
    fk_ref[...] = fk
    fkb_ref[...] = fk.astype(BF16)
    fv = pm[:, 2 * w:3 * w]
    fv_ref[...] = fv
    fvb_ref[...] = fv.astype(BF16)
    sqb_ref[...] = (pm[:, 3 * w:4 * w] * qscale).astype(BF16)
    sk = pm[:, 4 * w:5 * w]
    sk_ref[...] = sk
    skb_ref[...] = sk.astype(BF16)
    sv = pm[:, 5 * w:6 * w]
    sv_ref[...] = sv
    svb_ref[...] = sv.astype(BF16)
    fl = jnp.dot(h, wf_ref[...], preferred_element_type=F32) + bf_ref[...]
    lf = -_softplus(-fl)
    lf_ref[...] = lf[:, :nh]
    row = lax.broadcasted_iota(jnp.int32, (tm, tm), 0)
    col = lax.broadcasted_iota(jnp.int32, (tm, tm), 1)
    msk = col <= row
    if seg < tm:
        sh = int(math.log2(seg))
        msk = msk & (lax.shift_right_logical(col, sh) == lax.shift_right_logical(row, sh))
    mb = msk.astype(BF16)
    hi, mid, lo = _split3(lf)
    cum = (jnp.dot(mb, hi, preferred_element_type=F32) + jnp.dot(mb, mid, preferred_element_type=F32)
           + jnp.dot(mb, lo, preferred_element_type=F32))
    if tps > 1:
        @pl.when(i % tps == 0)
        def _():
            carry_scr[...] = jnp.zeros_like(carry_scr)
        cum = cum + carry_scr[...]
        carry_scr[...] = cum[tm - 1:tm]
    fc_ref[...] = cum[:, :nh]


def _cd_in(x, modg, j0, tpb, tm, wm, wf, bf, seq, nh):
    n, d = x.shape
    r = modg.shape[1]
    w = wm.shape[1] // 6
    hd = w // nh
    seg = min(seq, tm)
    assert seg == tm or (seg & (seg - 1)) == 0
    tps = max(seq // tm, 1)
    wide = lambda dt: jax.ShapeDtypeStruct((n, w), dt)
    wspec = pl.BlockSpec((tm, w), lambda i: (i, 0))
    nspec = pl.BlockSpec((tm, nh), lambda i: (i, 0))
    return pl.pallas_call(
        functools.partial(_cdin_kernel, w=w, nh=nh, seg=seg, tps=tps, qscale=hd ** -0.5),
        grid=(n // tm,),
        in_specs=[pl.BlockSpec((tm, d), lambda i: (i, 0)),
                  _mod_spec(r, j0, tpb, d), _mod_spec(r, j0 + 1, tpb, d),
                  _const_spec(wm.shape), _const_spec(wf.shape), _const_spec(bf.shape)],
        out_specs=[wspec] * 10 + [nspec, nspec],
        out_shape=[wide(BF16), wide(F32), wide(BF16), wide(F32), wide(BF16),
                   wide(BF16), wide(F32), wide(BF16), wide(F32), wide(BF16),
                   jax.ShapeDtypeStruct((n, nh), F32), jax.ShapeDtypeStruct((n, nh), F32)],
        scratch_shapes=[pltpu.VMEM((1, LANES), F32)],
        compiler_params=_cparams("arbitrary"),
        name="cd_in",
    )(x, modg, modg, wm, wf, bf)


def _fox_kernel(q_ref, k_ref, v_ref, fq_ref, fk_ref, o_ref, *, t, hd):
    qi = pl.program_id(2)
    q = q_ref[...]
    lane = lax.broadcasted_iota(jnp.int32, (1, 2 * hd), 1)
    row = lax.broadcasted_iota(jnp.int32, (t, t), 0)
    col = lax.broadcasted_iota(jnp.int32, (t, t), 1)
    outs = []
    for hh in range(2):
        sel = (lane >= hh * hd) & (lane < (hh + 1) * hd)
        qm = jnp.where(sel, q, jnp.zeros_like(q))
        ft = fq_ref[0, :, hh:hh + 1]

        def block(ki, carry, masked):
            m, l, acc = carry
            ks = pl.multiple_of(ki * t, t)
            k = k_ref[pl.ds(ks, t), :]
            v = v_ref[pl.ds(ks, t), :]
            s = _bdot_nt(qm, k) + ft - fk_ref[0, 0, hh:hh + 1, pl.ds(ks, t)]
            if masked:
                s = jnp.where(col <= row, s, -jnp.inf)
            mn = jnp.maximum(m, jnp.max(s, axis=-1, keepdims=True))
            alpha = jnp.exp(m - mn)
            p = jnp.exp(s - mn)
            l = alpha * l + jnp.sum(p, axis=-1, keepdims=True)
            acc = alpha * acc + jnp.dot(p.astype(BF16), v, preferred_element_type=F32)
            return mn, l, acc

        init = (jnp.full((t, 1), -jnp.inf, F32), jnp.zeros((t, 1), F32), jnp.zeros((t, 2 * hd), F32))
        carry = lax.fori_loop(0, qi, functools.partial(block, masked=False), init)
        m, l, acc = block(qi, carry, True)
        outs.append(acc / l)
    o_ref[...] = jnp.where(lane < hd, outs[0], outs[1]).astype(o_ref.dtype)


def _sb_kernel(bsb_ref, q_ref, k_ref, v_ref, o_ref, *, t, hd):
    pair = pl.program_id(1)
    qi = pl.program_id(2)
    q = q_ref[...]
    lane = lax.broadcasted_iota(jnp.int32, (1, 2 * hd), 1)
    row = lax.broadcasted_iota(jnp.int32, (t, t), 0)
    col = lax.broadcasted_iota(jnp.int32, (t, t), 1)
    suffix = (row > col).astype(BF16)
    strict = col < row
    outs = []
    for hh in range(2):
        sel = (lane >= hh * hd) & (lane < (hh + 1) * hd)
        qm = jnp.where(sel, q, jnp.zeros_like(q))
        bias = bsb_ref[2 * pair + hh]

        def block(ki, carry, masked):
            rsum, acc = carry
            ks = pl.multiple_of(ki * t, t)
            k = k_ref[pl.ds(ks, t), :]
            v = v_ref[pl.ds(ks, t), :]
            z = _bdot_nt(qm, k) + bias
            sp = _softplus(z)
            lk = -sp
            if masked:
                lk = jnp.where(strict, lk, 0.0)
            lh, ll = _split2(lk)
            later = (jnp.dot(lh, suffix, preferred_element_type=F32)
                     + jnp.dot(ll, suffix, preferred_element_type=F32))
            wgt = jnp.exp(z - sp + later + rsum)
            if masked:
                wgt = jnp.where(strict, wgt, 0.0)
            rsum = rsum + later[:, 0:1] + lk[:, 0:1]
            acc = acc + jnp.dot(wgt.astype(BF16), v, preferred_element_type=F32)
            return rsum, acc

        carry = block(qi, (jnp.zeros((t, 1), F32), jnp.zeros((t, 2 * hd), F32)), True)
        rsum, acc = lax.fori_loop(0, qi, lambda j, cr: block(qi - 1 - j, cr, False), carry)
        outs.append(acc)
    o_ref[...] = jnp.where(lane < hd, outs[0], outs[1]).astype(o_ref.dtype)


def _attn_prompt(fqb, fkb, fvb, fcum, sqb, skb, svb, b_sb, nb, seq, nh, t):
    n, w = fqb.shape
    hd = w // nh
    npair = nh // 2
    nq = seq // t
    fc_q = fcum.reshape(n, npair, 2).transpose(1, 0, 2)
    fc_k = fcum.reshape(nb, seq, npair, 2).transpose(0, 2, 3, 1)
    qspec = pl.BlockSpec((t, 2 * hd), lambda b, p, i, *_: (b * nq + i, p))
    kspec = pl.BlockSpec((seq, 2 * hd), lambda b, p, i, *_: (b, p))
    fo = pl.pallas_call(
        functools.partial(_fox_kernel, t=t, hd=hd),
        grid=(nb, npair, nq),
        in_specs=[qspec, kspec, kspec,
                  pl.BlockSpec((1, t, 2), lambda b, p, i: (p, b * nq + i, 0)),
                  pl.BlockSpec((1, 1, 2, seq), lambda b, p, i: (b, p, 0, 0))],
        out_specs=qspec,
        out_shape=jax.ShapeDtypeStruct((n, w), BF16),
        compiler_params=_cparams("parallel", "parallel", "arbitrary"),
        name="fox_prompt",
    )(fqb, fkb, fvb, fc_q, fc_k)
    so = pl.pallas_call(
        functools.partial(_sb_kernel, t=t, hd=hd),
        grid_spec=pltpu.PrefetchScalarGridSpec(
            num_scalar_prefetch=1, grid=(nb, npair, nq),
            in_specs=[qspec, kspec, kspec], out_specs=qspec),
        out_shape=jax.ShapeDtypeStruct((n, w), BF16),
        compiler_params=_cparams("parallel", "parallel", "arbitrary"),
        name="sb_prompt",
    )(b_sb.astype(F32), sqb, skb, svb)
    return fo, so


def _attn_sample_kernel(pt_ref, q_f_ref, kn_f_ref, vn_f_ref, q_s_ref, kn_s_ref, vn_s_ref,
                        fcol_ref, frow_ref, bcol_ref, kp_f_ref, vp_f_ref, lp_ref, kp_s_ref, vp_s_ref,
                        fo_ref, so_ref, m_scr, l_scr, af_scr, r_scr, as_scr, c_scr, *, nh, hd, nt, ps):
    p = pl.program_id(1)
    rows = nh * nt
    w = nh * hd
    rhead = lax.shift_right_logical(lax.broadcasted_iota(jnp.int32, (rows, w), 0), int(math.log2(nt)))
    lhead = lax.shift_right_logical(lax.broadcasted_iota(jnp.int32, (rows, w), 1), int(math.log2(hd)))
    own = rhead == lhead
    tok = lax.broadcasted_iota(jnp.int32, (rows, ps), 0) & (nt - 1)
    key = lax.broadcasted_iota(jnp.int32, (rows, ps), 1)
    jr = lax.broadcasted_iota(jnp.int32, (ps, ps), 0)
    sr = lax.broadcasted_iota(jnp.int32, (ps, ps), 1)
    suffix = (jr > sr).astype(BF16)

    def qbd(q_ref):
        q = q_ref[...]
        return jnp.where(own, jnp.concatenate([q] * nh, axis=0), 0.0).astype(BF16)

    def fox_update(k, v, bias, mask):
        s = _bdot_nt(qbd(q_f_ref), k) + bias
        if mask is not None:
            s = jnp.where(mask, s, -jnp.inf)
        m = m_scr[...]
        mn = jnp.maximum(m, jnp.max(s, axis=-1, keepdims=True))
        alpha = jnp.exp(m - mn)
        pr = jnp.exp(s - mn)
        l_scr[...] = alpha * l_scr[...] + jnp.sum(pr, axis=-1, keepdims=True)
        af_scr[...] = alpha * af_scr[...] + _bdot(pr, v)
        m_scr[...] = mn

    def sb_update(k, v, mask):
        z = _bdot_nt(qbd(q_s_ref), k) + bcol_ref[...]
        sp = _softplus(z)
        lk = -sp
        if mask is not None:
            lk = jnp.where(mask, lk, 0.0)
        lh, ll = _split2(lk)
        later = (jnp.dot(lh, suffix, preferred_element_type=F32)
                 + jnp.dot(ll, suffix, preferred_element_type=F32))
        wgt = jnp.exp(z - sp + later + r_scr[...])
        if mask is not None:
            wgt = jnp.where(mask, wgt, 0.0)
        r_scr[...] = r_scr[...] + later[:, 0:1] + lk[:, 0:1]
        as_scr[...] = as_scr[...] + _bdot(wgt, v)

    @pl.when(p == 0)
    def _():
        m_scr[...] = jnp.full_like(m_scr, -jnp.inf)
        l_scr[...] = jnp.zeros_like(l_scr)
        af_scr[...] = jnp.zeros_like(af_scr)
        r_scr[...] = jnp.zeros_like(r_scr)
        as_scr[...] = jnp.zeros_like(as_scr)
        c_scr[...] = jnp.zeros_like(c_scr)
        fox_update(kn_f_ref[0], vn_f_ref[0], fcol_ref[0] - frow_ref[0], key <= tok)
        sb_update(kn_s_ref[0], vn_s_ref[0], key < tok)

    @pl.when(p > 0)
    def _():
        lp = lp_ref[0]
        hi, mid, lo = _split3(lp)
        suf = (jnp.dot(hi, suffix, preferred_element_type=F32) + jnp.dot(mid, suffix, preferred_element_type=F32)
               + jnp.dot(lo, suffix, preferred_element_type=F32)) + c_scr[...]
        c_scr[...] = c_scr[...] + jnp.sum(lp, axis=-1, keepdims=True)
        fexp = jnp.concatenate([jnp.broadcast_to(suf[h:h + 1], (nt, ps)) for h in range(nh)], axis=0)
        fox_update(kp_f_ref[0], vp_f_ref[0], fcol_ref[0] + fexp, None)
        sb_update(kp_s_ref[0], vp_s_ref[0], None)

    @pl.when(p == pl.num_programs(1) - 1)
    def _():
        def extract(acc):
            out = jnp.zeros((nt, w), F32)
            for h in range(nh):
                out = out + jnp.where(own[h * nt:(h + 1) * nt], acc[h * nt:(h + 1) * nt], 0.0)
            return out
        fo_ref[...] = extract(af_scr[...] / l_scr[...]).astype(fo_ref.dtype)
        so_ref[...] = extract(as_scr[...]).astype(so_ref.dtype)


def _attn_sample(fq, fk, fv, fcum, sq, sk, sv, b_sb, pools, page_table, nb, nt, nh):
    pool_fk, pool_fv, pool_lf, pool_sk, pool_sv = pools
    n, w = fq.shape
    hd = w // nh
    nphys, ps = pool_fk.shape[:2]
    npg = page_table.shape[1]
    rows = nh * nt
    assert (nt & (nt - 1)) == 0 and (hd & (hd - 1)) == 0 and nt <= ps

    def padpage(a):
        return jnp.pad(a.reshape(nb, nt, w), ((0, 0), (0, ps - nt), (0, 0)))

    fct = fcum.reshape(nb, nt, nh).transpose(0, 2, 1)
    fcol = fct.reshape(nb, rows, 1)
    frow = jnp.pad(jnp.repeat(fct, nt, axis=1), ((0, 0), (0, 0), (0, ps - nt)))
    bcol = jnp.repeat(b_sb.astype(F32), nt).reshape(rows, 1)
    lpt = pool_lf.astype(F32).transpose(0, 2, 1)
    flat = lambda a: a.reshape(nphys, ps, w)
    pt = page_table.reshape(-1).astype(jnp.int32)

    qspec = pl.BlockSpec((nt, w), lambda b, p, pt: (b, 0))
    nspec = pl.BlockSpec((1, ps, w), lambda b, p, pt: (b, 0, 0))
    page = lambda b, p, pt: (pt[b * npg + npg - jnp.maximum(p, 1)], 0, 0)
    pspec = pl.BlockSpec((1, ps, w), page)
    return pl.pallas_call(
        functools.partial(_attn_sample_kernel, nh=nh, hd=hd, nt=nt, ps=ps),
        grid_spec=pltpu.PrefetchScalarGridSpec(
            num_scalar_prefetch=1, grid=(nb, npg + 1),
            in_specs=[qspec, nspec, nspec, qspec, nspec, nspec,
                      pl.BlockSpec((1, rows, 1), lambda b, p, pt: (b, 0, 0)),
                      pl.BlockSpec((1, rows, ps), lambda b, p, pt: (b, 0, 0)),
                      pl.BlockSpec((rows, 1), lambda b, p, pt: (0, 0)),
                      pspec, pspec, pl.BlockSpec((1, nh, ps), page), pspec, pspec],
            out_specs=[qspec, qspec],
            scratch_shapes=[pltpu.VMEM((rows, 1), F32), pltpu.VMEM((rows, 1), F32), pltpu.VMEM((rows, w), F32),
                            pltpu.VMEM((rows, 1), F32), pltpu.VMEM((rows, w), F32), pltpu.VMEM((nh, 1), F32)]),
        out_shape=[jax.ShapeDtypeStruct((n, w), F32), jax.ShapeDtypeStruct((n, w), F32)],
        compiler_params=_cparams("parallel", "arbitrary"),
        name="attn_sample",
    )(pt, fq, padpage(fk), padpage(fv), sq, padpage(sk), padpage(sv), fcol, frow, bcol,
      flat(pool_fk), flat(pool_fv), lpt, flat(pool_sk), flat(pool_sv))


def _trunk(x3, mod, ab_state, past, page_table, wts, t_attn):
    nb, seq, d = x3.shape
    n = nb * seq
    x = x3.reshape(n, d)
    tm = _tile(n, 256)
    if seq % tm == 0:
        modg = [mod[l][:, None, :] for l in range(2)]
        tpb = seq // tm
    else:
        assert tm % seq == 0
        modg = [jnp.repeat(mod[l], seq, axis=0).reshape(n // tm, tm, -1) for l in range(2)]
        tpb = 1
    nh = wts['nh_att']

    x = _ffn(x, modg[0], 0, tpb, tm, wts['ffn_in'][0][0], wts['ffn_out'][0][0])
    proj = _proj(x, modg[0], 3, tpb, tm, wts['ab_w_in'])
    w5 = wts['s5_glu_w'].shape[0]
    ys, s5_re, s5_im = _s5(proj[:, :w5], ab_state[0], ab_state[1], wts['s5'], nb, seq)
    oh, hg = _hgrn(proj, w5, wts['hgrn_lb'], wts['hgrn_norm_w'], ab_state[2], nb, seq)
    x = _ab_out(x, modg[0], 5, tpb, tm, ys, oh, wts['s5_glu_w'], wts['s5_glu_b'], wts['ab_w_out'])
    x = _ffn(x, modg[0], 6, tpb, tm, wts['ffn_in'][0][1], wts['ffn_out'][0][1])

    x = _ffn(x, modg[1], 0, tpb, tm, wts['ffn_in'][1][0], wts['ffn_out'][1][0])
    (fqb, fk, fkb, fv, fvb, sqb, sk, skb, sv, svb, logf, fcum) = _cd_in(
        x, modg[1], 3, tpb, tm, wts['cd_wm'], wts['cd_wf'], wts['cd_bf'], seq, nh)
    if past is None:
        fo, so = _attn_prompt(fqb, fkb, fvb, fcum, sqb, skb, svb, wts['cd_b_sb'], nb, seq, nh, t_attn)
    else:
        fo, so = _attn_sample(fqb.astype(F32), fk, fv, fcum, sqb.astype(F32), sk, sv, wts['cd_b_sb'],
                              past, page_table, nb, seq, nh)
    x = _cd_out(x, modg[1], 5, tpb, tm, fo, so, wts['cd_w_out'])
    y = _ffn(x, modg[1], 6, tpb, tm, wts['ffn_in'][1][1], wts['ffn_out'][1][1], wts['final_norm_w'])

    hd = fk.shape[1] // nh
    kv = lambda a: a.reshape(nb, seq, nh, hd)
    return (y.reshape(nb, seq, d),
            (s5_re, s5_im, hg, kv(fk), kv(fv), logf.reshape(nb, seq, nh), kv(sk), kv(sv)))


def kernel(x_prompt, x_sample, state_s5_re, state_s5_im, state_hgrn, cache_fox_k, cache_fox_v,
           cache_fox_logf, cache_sb_k, cache_sb_v, page_table, c_prompt, c_sample, ada_w, ada_b,
           ffn_w_in, ffn_w_out, ab_w_in, ab_w_out, s5_a_re, s5_a_im, s5_log_dt, s5_b_re, s5_b_im,
           s5_c_re, s5_c_im, s5_d, s5_glu_w, s5_glu_b, hgrn_lb_logits, hgrn_norm_w, cd_w_in, cd_b_f,
           cd_b_sb, cd_w_out, final_norm_w, attn_tile=512):
    bp, seq_p, d = x_prompt.shape
    bs = x_sample.shape[0]
    nh_att, hd = cache_fox_k.shape[2:]
    wa = nh_att * hd
    g, p = s5_a_re.shape
    nhh, dk, dv = state_hgrn.shape[1:]

    nc_all = bp + bs
    c_all = jnp.pad(jnp.concatenate([c_prompt, c_sample], axis=0), ((0, (-nc_all) % 8), (0, 0)))
    mod = _ada(c_all, ada_w, ada_b)
    mod_p = [mod[l, :bp] for l in range(2)]
    mod_s = [mod[l, bp:nc_all] for l in range(2)]

    lb = jnp.cumsum(jax.nn.softmax(hgrn_lb_logits.astype(F32), axis=0), axis=0)[0].reshape(1, nhh * dk)

    wts = dict(
        nh_att=nh_att,
        ffn_in=[[ffn_w_in[l, s].astype(BF16) for s in range(2)] for l in range(2)],
        ffn_out=[[ffn_w_out[l, s].astype(BF16) for s in range(2)] for l in range(2)],
        ab_w_in=ab_w_in.astype(BF16), ab_w_out=ab_w_out.astype(BF16),
        s5=(s5_a_re, s5_a_im, s5_log_dt, s5_b_re, s5_b_im, s5_c_re, s5_c_im, s5_d),
        s5_glu_w=s5_glu_w.astype(BF16), s5_glu_b=s5_glu_b, hgrn_lb=lb, hgrn_norm_w=hgrn_norm_w,
        cd_wm=jnp.concatenate([cd_w_in[:, :3 * wa], cd_w_in[:, 3 * wa + nh_att:]], axis=1).astype(BF16),
        cd_wf=jnp.pad(cd_w_in[:, 3 * wa:3 * wa + nh_att], ((0, 0), (0, LANES - nh_att))).astype(BF16),
        cd_bf=jnp.pad(cd_b_f.astype(F32), (0, LANES - nh_att)).reshape(1, LANES),
        cd_b_sb=cd_b_sb, cd_w_out=cd_w_out.astype(BF16), final_norm_w=final_norm_w)

    zeros = lambda *s: jnp.zeros(s, F32)
    y_p, st_p = _trunk(x_prompt, mod_p, (zeros(bp, g, p), zeros(bp, g, p), zeros(bp, nhh, dk, dv)),
                       None, None, wts, _tile(seq_p, attn_tile))
    y_s, st_s = _trunk(x_sample, mod_s, (state_s5_re, state_s5_im, state_hgrn),
                       (cache_fox_k, cache_fox_v, cache_fox_logf, cache_sb_k, cache_sb_v),
                       page_table, wts, None)
    return (y_p, y_s) + st_p + st_s
```

```python
import functools
import math

import jax
import jax.numpy as jnp
from jax import lax
from jax.experimental import pallas as pl
from jax.experimental.pallas import tpu as pltpu

F32 = jnp.float32
BF16 = jnp.bfloat16
EPS = 1e-6
N_ADA = 9
S5_GROUP_CH = 16
S5_CHUNK = 64
HGRN_CHUNK = 64
LANES = 128
VMEM_LIMIT = 56 * 1024 * 1024


def _cparams(*sem):
    return pltpu.CompilerParams(dimension_semantics=sem, vmem_limit_bytes=VMEM_LIMIT)


def _tile(n, pref):
    if n <= pref:
        return n
    t = pref - pref % 8
    while n % t:
        t -= 8
    return t


def _bdot(a, b):
    return jnp.dot(a.astype(BF16), b.astype(BF16), preferred_element_type=F32)


def _bdot_nt(a, b):
    return lax.dot_general(a.astype(BF16), b.astype(BF16), (((1,), (1,)), ((), ())),
                           preferred_element_type=F32)


def _bdot_tn(a, b):
    return lax.dot_general(a.astype(BF16), b.astype(BF16), (((0,), (0,)), ((), ())),
                           preferred_element_type=F32)


def _split2(x):
    hi = x.astype(BF16)
    lo = (x - hi.astype(F32)).astype(BF16)
    return hi, lo


def _split3(x):
    hi = x.astype(BF16)
    r = x - hi.astype(F32)
    mid = r.astype(BF16)
    lo = (r - mid.astype(F32)).astype(BF16)
    return hi, mid, lo


def _sigmoid(x):
    return 1.0 / (1.0 + jnp.exp(-x))


def _silu(x):
    return x * _sigmoid(x)


def _softplus(x):
    return jnp.maximum(x, 0.0) + jnp.log(1.0 + jnp.exp(-jnp.abs(x)))


def _rms(x):
    return x * lax.rsqrt(jnp.mean(x * x, axis=-1, keepdims=True) + EPS)


def _modulate(x, shift, scale):
    return _rms(x) * (1.0 + scale) + shift


def _mod_spec(r, j, tpb, d):
    return pl.BlockSpec((1, r, d), lambda i: (i // tpb, 0, j))


def _const_spec(shape):
    nd = len(shape)
    return pl.BlockSpec(shape, lambda *_: (0,) * nd, pipeline_mode=pl.Buffered(1))


def _ada_kernel(c_ref, w_ref, b_ref, o_ref):
    s = _silu(c_ref[...])
    w = w_ref[0]
    sh, sl = _split2(s)
    wh, wl = _split2(w)
    acc = jnp.dot(sh, wh, preferred_element_type=F32)
    acc += jnp.dot(sh, wl, preferred_element_type=F32)
    acc += jnp.dot(sl, wh, preferred_element_type=F32)
    o_ref[0] = acc + b_ref[0]


def _ada(c_all, ada_w, ada_b):
    depth, d, nout = ada_w.shape
    m = c_all.shape[0]
    tn = _tile(nout, 1152) if nout % LANES == 0 else nout
    return pl.pallas_call(
        _ada_kernel,
        grid=(depth, nout // tn),
        in_specs=[pl.BlockSpec((m, d), lambda l, j: (0, 0)),
                  pl.BlockSpec((1, d, tn), lambda l, j: (l, 0, j)),
                  pl.BlockSpec((1, 1, tn), lambda l, j: (l, 0, j))],
        out_specs=pl.BlockSpec((1, m, tn), lambda l, j: (l, 0, j)),
        out_shape=jax.ShapeDtypeStruct((depth, m, nout), F32),
        compiler_params=_cparams("arbitrary", "arbitrary"),
        name="ada",
    )(c_all, ada_w, ada_b.reshape(depth, 1, nout))


def _ffn_kernel(*refs, dff, final):
    if final:
        x_ref, sh_ref, sc_ref, gt_ref, win_ref, wout_ref, fw_ref, o_ref = refs
    else:
        x_ref, sh_ref, sc_ref, gt_ref, win_ref, wout_ref, o_ref = refs
    x = x_ref[...]
    h = _modulate(x, sh_ref[0], sc_ref[0]).astype(BF16)
    gu = jnp.dot(h, win_ref[...], preferred_element_type=F32)
    g = gu[:, :dff]
    u = gu[:, dff:]
    a = (_silu(g) * u).astype(BF16)
    out = jnp.dot(a, wout_ref[...], preferred_element_type=F32)
    y = x + 0.5 * gt_ref[0] * out
    if final:
        y = _rms(y) * fw_ref[...]
    o_ref[...] = y


def _ffn(x, modg, j0, tpb, tm, w_in, w_out, final_w=None):
    n, d = x.shape
    r = modg.shape[1]
    dff = w_out.shape[0]
    final = final_w is not None
    in_specs = [pl.BlockSpec((tm, d), lambda i: (i, 0)),
                _mod_spec(r, j0, tpb, d), _mod_spec(r, j0 + 1, tpb, d), _mod_spec(r, j0 + 2, tpb, d),
                _const_spec(w_in.shape), _const_spec(w_out.shape)]
    args = [x, modg, modg, modg, w_in, w_out]
    if final:
        in_specs.append(_const_spec((1, d)))
        args.append(final_w.reshape(1, d))
    return pl.pallas_call(
        functools.partial(_ffn_kernel, dff=dff, final=final),
        grid=(n // tm,),
        in_specs=in_specs,
        out_specs=pl.BlockSpec((tm, d), lambda i: (i, 0)),
        out_shape=jax.ShapeDtypeStruct((n, d), F32),
        compiler_params=_cparams("parallel"),
        name="ffn",
    )(*args)


def _proj_kernel(x_ref, sh_ref, sc_ref, w_ref, o_ref):
    h = _modulate(x_ref[...], sh_ref[0], sc_ref[0]).astype(BF16)
    o_ref[...] = jnp.dot(h, w_ref[...], preferred_element_type=F32)


def _proj(x, modg, j0, tpb, tm, w):
    n, d = x.shape
    r = modg.shape[1]
    nout = w.shape[1]
    return pl.pallas_call(
        _proj_kernel,
        grid=(n // tm,),
        in_specs=[pl.BlockSpec((tm, d), lambda i: (i, 0)),
                  _mod_spec(r, j0, tpb, d), _mod_spec(r, j0 + 1, tpb, d),
                  _const_spec(w.shape)],
        out_specs=pl.BlockSpec((tm, nout), lambda i: (i, 0)),
        out_shape=jax.ShapeDtypeStruct((n, nout), F32),
        compiler_params=_cparams("parallel"),
        name="ab_in",
    )(x, modg, modg, w)


def _cmul_rows(x, a1, a2):
    return x * a1 + pltpu.roll(x, x.shape[-1] // 2, 1) * a2


def _s5_kernel(u_ref, tb_ref, sm_ref, ym_ref, a1_ref, a2_ref, h0_ref, y_ref, hf_ref, tm_scr, *, nb, nc):
    nblk = tb_ref.shape[1]

    @pl.when(pl.program_id(0) == 0)
    def _():
        tm_scr[...] = jnp.zeros_like(tm_scr)

    for s in range(nblk):
        for t in range(s, nblk):
            tm_scr[s * LANES:(s + 1) * LANES, t * LANES:(t + 1) * LANES] = tb_ref[0, t - s]
    u = u_ref[0].astype(BF16)
    e = jnp.dot(u, sm_ref[0], preferred_element_type=F32)
    a1 = a1_ref[0]
    a2 = a2_ref[0]
    h0 = h0_ref[0]
    if nc == 1:
        hprev = h0
        hf_ref[0] = _cmul_rows(h0, a1[0:1], a2[0:1]) + e
    else:
        row = lax.broadcasted_iota(jnp.int32, (nc, e.shape[1]), 0)
        prevs = []
        for b in range(nb):
            h0b = h0[b:b + 1]
            x = e[b * nc:(b + 1) * nc]
            x = x + jnp.where(row == 0, _cmul_rows(h0b, a1[0:1], a2[0:1]), 0.0)
            d, k = 1, 0
            while d < nc:
                sh = jnp.where(row >= d, pltpu.roll(x, d, 0), 0.0)
                x = x + _cmul_rows(sh, a1[k:k + 1], a2[k:k + 1])
                d, k = 2 * d, k + 1
            hf_ref[0, b:b + 1, :] = x[nc - 1:nc]
            prevs.append(jnp.where(row == 0, h0b, pltpu.roll(x, 1, 0)))
        hprev = jnp.concatenate(prevs, axis=0)
    y_ref[0] = (jnp.dot(u, tm_scr[...], preferred_element_type=F32)
                + _bdot(hprev, ym_ref[0]))


def _s5_params(a_re, a_im, log_dt, b_re, b_im, c_re, c_im, d_skip, t, nc):
    hp = lax.Precision.HIGHEST
    g, p = a_re.shape
    hc = b_re.shape[2]
    dt = jnp.exp(log_dt.astype(F32))[:, None]
    lam_re, lam_im = a_re.astype(F32), a_im.astype(F32)
    mag = jnp.exp(lam_re * dt)
    lbar_re = mag * jnp.cos(lam_im * dt)
    lbar_im = mag * jnp.sin(lam_im * dt)
    den = lam_re * lam_re + lam_im * lam_im
    nr = lbar_re - 1.0
    zr = (nr * lam_re + lbar_im * lam_im) / den
    zi = (lbar_im * lam_re - nr * lam_im) / den
    bb_re = zr[..., None] * b_re - zi[..., None] * b_im
    bb_im = zr[..., None] * b_im + zi[..., None] * b_re

    def powers(exps):
        ef = jnp.asarray(exps, F32)[:, None, None]
        m = jnp.exp(lam_re * dt * ef)
        return m * jnp.cos(lam_im * dt * ef), m * jnp.sin(lam_im * dt * ef)

    pr, pi = powers(list(range(t + 1)))
    pb_re = pr[:t, :, :, None] * bb_re - pi[:t, :, :, None] * bb_im
    pb_im = pr[:t, :, :, None] * bb_im + pi[:t, :, :, None] * bb_re
    kk = (jnp.einsum('gop,tgpi->gtio', c_re, pb_re, precision=hp)
          - jnp.einsum('gop,tgpi->gtio', c_im, pb_im, precision=hp))
    kk = kk.at[:, 0].add(d_skip[:, :, None] * jnp.eye(hc, dtype=F32))
    sb = LANES // hc
    nblk = t // sb
    lag = (sb * jnp.arange(nblk)[:, None, None] + jnp.arange(sb)[None, None, :]
           - jnp.arange(sb)[None, :, None])
    tb = jnp.where((lag >= 0)[None, :, :, :, None, None], kk[:, jnp.clip(lag, 0, t - 1)], 0.0)
    tb = tb.transpose(0, 1, 2, 4, 3, 5).reshape(g, nblk, LANES, LANES)
    sm = jnp.concatenate([pb_re[::-1].transpose(1, 0, 3, 2), pb_im[::-1].transpose(1, 0, 3, 2)], axis=-1)
    sm = sm.reshape(g, t * hc, 2 * p)
    p1r, p1i = pr[1:], pi[1:]
    y_r = c_re[None] * p1r[:, :, None, :] - c_im[None] * p1i[:, :, None, :]
    y_i = -c_re[None] * p1i[:, :, None, :] - c_im[None] * p1r[:, :, None, :]
    ym = jnp.concatenate([y_r, y_i], axis=-1).transpose(1, 3, 0, 2).reshape(g, 2 * p, t * hc)
    nk = max(1, int(math.log2(nc))) if nc > 1 else 1
    ar, ai = powers([t * (2 ** k) for k in range(nk)])
    a1 = jnp.concatenate([ar, ar], axis=-1).transpose(1, 0, 2)
    a2 = jnp.concatenate([-ai, ai], axis=-1).transpose(1, 0, 2)
    return tb.astype(BF16), sm.astype(BF16), ym.astype(BF16), a1, a2


def _s5(u, h0_re, h0_im, s5w, nb, seq):
    a_re, a_im, log_dt, b_re, b_im, c_re, c_im, d_skip = s5w
    g, p = a_re.shape
    hc = b_re.shape[2]
    t = min(S5_CHUNK, seq)
    nc = seq // t
    m = nb * nc
    tb, sm, ym, a1, a2 = _s5_params(a_re, a_im, log_dt, b_re, b_im, c_re, c_im, d_skip, t, nc)
    nk = a1.shape[1]
    nblk = tb.shape[1]
    uf = u.reshape(nb, nc, t, g, hc).transpose(3, 0, 1, 2, 4).reshape(g, m, t * hc)
    h0 = jnp.concatenate([h0_re, h0_im], axis=-1).transpose(1, 0, 2)
    y, hf = pl.pallas_call(
        functools.partial(_s5_kernel, nb=nb, nc=nc),
        grid=(g,),
        in_specs=[pl.BlockSpec((1, m, t * hc), lambda i: (i, 0, 0)),
                  pl.BlockSpec((1, nblk, LANES, LANES), lambda i: (i, 0, 0, 0)),
                  pl.BlockSpec((1, t * hc, 2 * p), lambda i: (i, 0, 0)),
                  pl.BlockSpec((1, 2 * p, t * hc), lambda i: (i, 0, 0)),
                  pl.BlockSpec((1, nk, 2 * p), lambda i: (i, 0, 0)),
                  pl.BlockSpec((1, nk, 2 * p), lambda i: (i, 0, 0)),
                  pl.BlockSpec((1, nb, 2 * p), lambda i: (i, 0, 0))],
        out_specs=[pl.BlockSpec((1, m, t * hc), lambda i: (i, 0, 0)),
                   pl.BlockSpec((1, nb, 2 * p), lambda i: (i, 0, 0))],
        out_shape=[jax.ShapeDtypeStruct((g, m, t * hc), F32),
                   jax.ShapeDtypeStruct((g, nb, 2 * p), F32)],
        scratch_shapes=[pltpu.VMEM((t * hc, t * hc), BF16)],
        compiler_params=_cparams("arbitrary"),
        name="s5",
    )(uf, tb, sm, ym, a1, a2, h0)
    y = y.reshape(g, nb, nc, t, hc).transpose(1, 2, 3, 0, 4).reshape(nb * seq, g * hc)
    hf = hf.transpose(1, 0, 2)
    return y, hf[..., :p], hf[..., p:]


def _hgrn_kernel(q_ref, f_ref, i_ref, g_ref, lb_ref, nw_ref, s0_ref, o_ref, sf_ref, st_scr, *, c, nch):
    li = pl.program_id(2)

    @pl.when(li == 0)
    def _():
        st_scr[...] = s0_ref[0, 0]

    lb = lb_ref[...]
    nw = nw_ref[...]
    row = lax.broadcasted_iota(jnp.int32, (c, c), 0)
    col = lax.broadcasted_iota(jnp.int32, (c, c), 1)
    causal = col <= row
    tril = causal.astype(BF16)
    mid = max(c // 2 - 1, 0)

    def chunk(ci, carry):
        sl = pl.ds(pl.multiple_of(ci * c, c), c)
        f = lb + (1.0 - lb) * _sigmoid(f_ref[sl, :])
        k = 1.0 - f
        q = _silu(q_ref[sl, :])
        v = i_ref[sl, :]
        gh, gl = _split2(jnp.log(f))
        b = (jnp.dot(tril, gh, preferred_element_type=F32)
             + jnp.dot(tril, gl, preferred_element_type=F32))
        r = b[mid:mid + 1]
        sc = jnp.where(causal, _bdot_nt(q * jnp.exp(b - r), k * jnp.exp(r - b)), 0.0)
        st = st_scr[...]
        o = _bdot_nt(q * jnp.exp(b), st) + _bdot(sc, v)
        bl = b[c - 1:c]
        st_scr[...] = st * jnp.exp(bl) + _bdot_tn(v, k * jnp.exp(bl - b))
        o_ref[sl, :] = _rms(o) * nw * _silu(g_ref[sl, :])
        return carry

    lax.fori_loop(0, nch, chunk, 0)

    @pl.when(li == pl.num_programs(2) - 1)
    def _():
        sf_ref[0, 0] = st_scr[...]


def _hgrn(proj, col0, lb, norm_w, s0, nb, seq):
    nh, dk, dv = s0.shape[1:]
    n = nb * seq
    c = min(HGRN_CHUNK, seq)
    tl = _tile(seq, 1024)
    nl = seq // tl
    cb = col0 // dk
    s0t = s0.astype(F32).transpose(0, 1, 3, 2)

    def colspec(j):
        return pl.BlockSpec((tl, dk), lambda b, h, l: (b * nl + l, cb + j * nh + h))

    o, sft = pl.pallas_call(
        functools.partial(_hgrn_kernel, c=c, nch=tl // c),
        grid=(nb, nh, nl),
        in_specs=[colspec(0), colspec(1), colspec(2), colspec(3),
                  pl.BlockSpec((1, dk), lambda b, h, l: (0, h)),
                  pl.BlockSpec((1, dv), lambda b, h, l: (0, 0)),
                  pl.BlockSpec((1, 1, dv, dk), lambda b, h, l: (b, h, 0, 0))],
        out_specs=[pl.BlockSpec((tl, dv), lambda b, h, l: (b * nl + l, h)),
                   pl.BlockSpec((1, 1, dv, dk), lambda b, h, l: (b, h, 0, 0))],
        out_shape=[jax.ShapeDtypeStruct((n, nh * dv), F32),
                   jax.ShapeDtypeStruct((nb, nh, dv, dk), F32)],
        scratch_shapes=[pltpu.VMEM((dv, dk), F32)],
        compiler_params=_cparams("parallel", "parallel", "arbitrary"),
        name="hgrn",
    )(proj, proj, proj, proj, lb, norm_w.reshape(1, dv), s0t)
    return o, sft.transpose(0, 1, 3, 2)


def _gelu_tanh(x):
    return x * (0.5 * (1.0 + jnp.tanh(math.sqrt(2.0 / math.pi) * (x + 0.044715 * (x * x * x)))))


def _about_kernel(x_ref, gt_ref, ys_ref, oh_ref, gw_ref, gb_ref, wo_ref, o_ref, *, w5):
    y = _gelu_tanh(ys_ref[...])
    s5 = y * _sigmoid(_bdot(y, gw_ref[...]) + gb_ref[...])
    out = _bdot(s5, wo_ref[:w5, :]) + _bdot(oh_ref[...], wo_ref[w5:, :])
    o_ref[...] = x_ref[...] + gt_ref[0] * out


def _ab_out(x, modg, j, tpb, tm, ys, oh, glu_w, glu_b, w_out):
    n, d = x.shape
    r = modg.shape[1]
    w5 = ys.shape[1]
    wh = oh.shape[1]
    return pl.pallas_call(
        functools.partial(_about_kernel, w5=w5),
        grid=(n // tm,),
        in_specs=[pl.BlockSpec((tm, d), lambda i: (i, 0)), _mod_spec(r, j, tpb, d),
                  pl.BlockSpec((tm, w5), lambda i: (i, 0)), pl.BlockSpec((tm, wh), lambda i: (i, 0)),
                  _const_spec(glu_w.shape), _const_spec((1, w5)), _const_spec(w_out.shape)],
        out_specs=pl.BlockSpec((tm, d), lambda i: (i, 0)),
        out_shape=jax.ShapeDtypeStruct((n, d), F32),
        compiler_params=_cparams("parallel"),
        name="ab_out",
    )(x, modg, ys, oh, glu_w, glu_b.reshape(1, w5), w_out)


def _cdout_kernel(x_ref, gt_ref, fo_ref, so_ref, wo_ref, o_ref, *, wf):
    out = _bdot(fo_ref[...], wo_ref[:wf, :]) + _bdot(so_ref[...], wo_ref[wf:, :])
    o_ref[...] = x_ref[...] + gt_ref[0] * out


def _cd_out(x, modg, j, tpb, tm, fo, so, w_out):
    n, d = x.shape
    r = modg.shape[1]
    wf = fo.shape[1]
    ws = so.shape[1]
    return pl.pallas_call(
        functools.partial(_cdout_kernel, wf=wf),
        grid=(n // tm,),
        in_specs=[pl.BlockSpec((tm, d), lambda i: (i, 0)), _mod_spec(r, j, tpb, d),
                  pl.BlockSpec((tm, wf), lambda i: (i, 0)), pl.BlockSpec((tm, ws), lambda i: (i, 0)),
                  _const_spec(w_out.shape)],
        out_specs=pl.BlockSpec((tm, d), lambda i: (i, 0)),
        out_shape=jax.ShapeDtypeStruct((n, d), F32),
        compiler_params=_cparams("parallel"),
        name="cd_out",
    )(x, modg, fo, so, w_out)


def _cdin_kernel(x_ref, sh_ref, sc_ref, wt_ref, wk_ref, wf_ref, bf_ref,
                 fkt_ref, fvt_ref, skt_ref, svt_ref, fqb_ref, fkb_ref, fvb_ref, sqb_ref, skb_ref, svb_ref,
                 lft_ref, fct_ref, carry_scr, *, w, seg, tps, qscale):
    i = pl.program_id(0)
    tm = x_ref.shape[0]
    h = _modulate(x_ref[...], sh_ref[0], sc_ref[0]).astype(BF16)
    pt = _bdot_nt(wt_ref[...], h)
    fkt_ref[0] = pt[0:w]
    fvt_ref[0] = pt[w:2 * w]
    skt_ref[0] = pt[2 * w:3 * w]
    svt_ref[0] = pt[3 * w:4 * w]
    pk = jnp.dot(h, wk_ref[...], preferred_element_type=F32)
    fqb_ref[...] = (pk[:, 0:w] * qscale).astype(BF16)
    fkb_ref[...] = pk[:, w:2 * w].astype(BF16)
    fvb_ref[...] = pk[:, 2 * w:3 * w].astype(BF16)
    sqb_ref[...] = (pk[:, 3 * w:4 * w] * qscale).astype(BF16)
    skb_ref[...] = pk[:, 4 * w:5 * w].astype(BF16)
    svb_ref[...] = pk[:, 5 * w:6 * w].astype(BF16)
    flt = _bdot_nt(wf_ref[...], h) + bf_ref[...]
    lft = -_softplus(-flt)
    lft_ref[0] = lft
    row = lax.broadcasted_iota(jnp.int32, (tm, tm), 0)
    col = lax.broadcasted_iota(jnp.int32, (tm, tm), 1)
    msk = row <= col
    if seg < tm:
        sh = int(math.log2(seg))
        msk = msk & (lax.shift_right_logical(col, sh) == lax.shift_right_logical(row, sh))
    mb = msk.astype(BF16)
    hi, mid, lo = _split3(lft)
    cum = (jnp.dot(hi, mb, preferred_element_type=F32) + jnp.dot(mid, mb, preferred_element_type=F32)
           + jnp.dot(lo, mb, preferred_element_type=F32))
    if tps > 1:
        @pl.when(i % tps == 0)
        def _():
            carry_scr[...] = jnp.zeros_like(carry_scr)
        cum = cum + carry_scr[...]
        carry_scr[...] = cum[:, tm - 1:tm]
    fct_ref[0] = cum


def _cd_in(x, modg, j0, tpb, tm, wt, wk, wf, bf, nb, seq):
    n, d = x.shape
    r = modg.shape[1]
    w = wt.shape[0] // 4
    nh = wf.shape[0]
    hd = w // nh
    seg = min(seq, tm)
    assert seg == tm or (seg & (seg - 1)) == 0
    tps = max(seq // tm, 1)
    nbk, lk = (nb, seq) if seq >= tm else (1, n)
    tpk = lk // tm
    tshape = lambda rows, dt: jax.ShapeDtypeStruct((nbk, rows, lk), dt)
    tspec = lambda rows: pl.BlockSpec((1, rows, tm), lambda i: (i // tpk, 0, i % tpk))
    kspec = pl.BlockSpec((tm, w), lambda i: (i, 0))
    return pl.pallas_call(
        functools.partial(_cdin_kernel, w=w, seg=seg, tps=tps, qscale=hd ** -0.5),
        grid=(n // tm,),
        in_specs=[pl.BlockSpec((tm, d), lambda i: (i, 0)),
                  _mod_spec(r, j0, tpb, d), _mod_spec(r, j0 + 1, tpb, d),
                  _const_spec(wt.shape), _const_spec(wk.shape), _const_spec(wf.shape), _const_spec(bf.shape)],
        out_specs=[tspec(w)] * 4 + [kspec] * 6 + [tspec(nh), tspec(nh)],
        out_shape=[tshape(w, F32)] * 4 + [jax.ShapeDtypeStruct((n, w), BF16)] * 6
                  + [tshape(nh, F32), tshape(nh, F32)],
        scratch_shapes=[pltpu.VMEM((nh, 1), F32)],
        compiler_params=_cparams("arbitrary"),
        name="cd_in",
    )(x, modg, modg, wt, wk, wf, bf)


def _head_lanes(q, hd):
    lane = lax.broadcasted_iota(jnp.int32, (1, 2 * hd), 1)
    return [jnp.where((lane >= hh * hd) & (lane < (hh + 1) * hd), q, jnp.zeros_like(q)) for hh in range(2)]


def _fox_kernel(q_ref, k_ref, v_ref, fk_ref, o_ref, *, t, hd):
    qi = pl.program_id(2)
    qm = _head_lanes(q_ref[...], hd)
    lane = lax.broadcasted_iota(jnp.int32, (1, 2 * hd), 1)
    row = lax.broadcasted_iota(jnp.int32, (t, t), 0)
    col = lax.broadcasted_iota(jnp.int32, (t, t), 1)
    outs = []
    for hh in range(2):

        def block(ki, carry, masked):
            m, l, acc = carry
            ks = pl.multiple_of(ki * t, t)
            k = k_ref[pl.ds(ks, t), :]
            v = v_ref[pl.ds(ks, t), :]
            s = _bdot_nt(qm[hh], k) - fk_ref[0, 0, hh:hh + 1, pl.ds(ks, t)]
            if masked:
                s = jnp.where(col <= row, s, -jnp.inf)
            mn = jnp.maximum(m, jnp.max(s, axis=-1, keepdims=True))
            alpha = jnp.exp(m - mn)
            p = jnp.exp(s - mn)
            l = alpha * l + jnp.sum(p, axis=-1, keepdims=True)
            acc = alpha * acc + jnp.dot(p.astype(BF16), v, preferred_element_type=F32)
            return mn, l, acc

        init = (jnp.full((t, 1), -jnp.inf, F32), jnp.zeros((t, 1), F32), jnp.zeros((t, 2 * hd), F32))
        carry = lax.fori_loop(0, qi, functools.partial(block, masked=False), init)
        m, l, acc = block(qi, carry, True)
        outs.append(acc / l)
    o_ref[...] = jnp.where(lane < hd, outs[0], outs[1]).astype(o_ref.dtype)


def _sb_kernel(bsb_ref, q_ref, k_ref, v_ref, o_ref, *, t, hd):
    pair = pl.program_id(1)
    qi = pl.program_id(2)
    qm = _head_lanes(q_ref[...], hd)
    lane = lax.broadcasted_iota(jnp.int32, (1, 2 * hd), 1)
    row = lax.broadcasted_iota(jnp.int32, (t, t), 0)
    col = lax.broadcasted_iota(jnp.int32, (t, t), 1)
    suffix = (row > col).astype(BF16)
    strict = col < row
    outs = []
    for hh in range(2):
        bias = bsb_ref[2 * pair + hh]

        def block(ki, carry, masked):
            rsum, acc = carry
            ks = pl.multiple_of(ki * t, t)
            k = k_ref[pl.ds(ks, t), :]
            v = v_ref[pl.ds(ks, t), :]
            z = _bdot_nt(qm[hh], k) + bias
            sp = _softplus(z)
            if masked:
                sp = jnp.where(strict, sp, 0.0)
            later = jnp.dot(sp.astype(BF16), suffix, preferred_element_type=F32)
            wgt = jnp.exp(z - sp - later - rsum)
            if masked:
                wgt = jnp.where(strict, wgt, 0.0)
            rsum = rsum + jnp.sum(sp, axis=-1, keepdims=True)
            acc = acc + jnp.dot(wgt.astype(BF16), v, preferred_element_type=F32)
            return rsum, acc

        carry = block(qi, (jnp.zeros((t, 1), F32), jnp.zeros((t, 2 * hd), F32)), True)
        rsum, acc = lax.fori_loop(0, qi, lambda j, cr: block(qi - 1 - j, cr, False), carry)
        outs.append(acc)
    o_ref[...] = jnp.where(lane < hd, outs[0], outs[1]).astype(o_ref.dtype)


def _attn_prompt(fqb, fkb, fvb, fct, sqb, skb, svb, b_sb, nb, seq, nh, t):
    n, w = fqb.shape
    hd = w // nh
    npair = nh // 2
    nq = seq // t
    qspec = pl.BlockSpec((t, 2 * hd), lambda b, p, i, *_: (b * nq + i, p))
    kspec = pl.BlockSpec((seq, 2 * hd), lambda b, p, i, *_: (b, p))
    fo = pl.pallas_call(
        functools.partial(_fox_kernel, t=t, hd=hd),
        grid=(nb, npair, nq),
        in_specs=[qspec, kspec, kspec,
                  pl.BlockSpec((1, 1, 2, seq), lambda b, p, i: (b, p, 0, 0))],
        out_specs=qspec,
        out_shape=jax.ShapeDtypeStruct((n, w), BF16),
        compiler_params=_cparams("parallel", "parallel", "arbitrary"),
        name="fox_prompt",
    )(fqb, fkb, fvb, fct.reshape(nb, npair, 2, seq))
    so = pl.pallas_call(
        functools.partial(_sb_kernel, t=t, hd=hd),
        grid_spec=pltpu.PrefetchScalarGridSpec(
            num_scalar_prefetch=1, grid=(nb, npair, nq),
            in_specs=[qspec, kspec, kspec], out_specs=qspec),
        out_shape=jax.ShapeDtypeStruct((n, w), BF16),
        compiler_params=_cparams("parallel", "parallel", "arbitrary"),
        name="sb_prompt",
    )(b_sb.astype(F32), sqb, skb, svb)
    return fo, so


def _attn_sample_kernel(pt_ref, q_f_ref, kn_f_ref, vn_f_ref, q_s_ref, kn_s_ref, vn_s_ref, frow_ref, bcol_ref,
                        *rest, nh, hd, nt, ps, gp):
    pages = rest[:5 * gp]
    fo_ref, so_ref = rest[5 * gp:5 * gp + 2]
    m_scr, l_scr, af_scr, r_scr, as_scr, c_scr = rest[5 * gp + 2:]
    p = pl.program_id(1)
    rows = nh * nt
    w = nh * hd
    rhead = lax.shift_right_logical(lax.broadcasted_iota(jnp.int32, (rows, w), 0), int(math.log2(nt)))
    lhead = lax.shift_right_logical(lax.broadcasted_iota(jnp.int32, (rows, w), 1), int(math.log2(hd)))
    own = rhead == lhead
    tok = lax.broadcasted_iota(jnp.int32, (rows, ps), 0) & (nt - 1)
    key = lax.broadcasted_iota(jnp.int32, (rows, ps), 1)

    def suffix(n):
        return (lax.broadcasted_iota(jnp.int32, (n, n), 0) > lax.broadcasted_iota(jnp.int32, (n, n), 1)).astype(BF16)

    def qbd(q_ref):
        q = q_ref[...]
        return jnp.where(own, jnp.concatenate([q] * nh, axis=0), 0.0).astype(BF16)

    def scores(q, kts):
        return jnp.concatenate([_bdot(q, kt) for kt in kts], axis=1)

    def weighted(pr, vts):
        out = _bdot_nt(pr[:, 0:ps], vts[0])
        for j in range(1, len(vts)):
            out = out + _bdot_nt(pr[:, j * ps:(j + 1) * ps], vts[j])
        return out

    def fox_update(kts, vts, bias, mask):
        s = scores(qbd(q_f_ref), kts) + bias
        if mask is not None:
            s = jnp.where(mask, s, -jnp.inf)
        m = m_scr[...]
        mn = jnp.maximum(m, jnp.max(s, axis=-1, keepdims=True))
        alpha = jnp.exp(m - mn)
        pr = jnp.exp(s - mn)
        l_scr[...] = alpha * l_scr[...] + jnp.sum(pr, axis=-1, keepdims=True)
        af_scr[...] = alpha * af_scr[...] + weighted(pr, vts)
        m_scr[...] = mn

    def sb_update(kts, vts, mask):
        z = scores(qbd(q_s_ref), kts) + bcol_ref[...]
        sp = _softplus(z)
        if mask is not None:
            sp = jnp.where(mask, sp, 0.0)
        later = jnp.dot(sp.astype(BF16), suffix(z.shape[1]), preferred_element_type=F32)
        wgt = jnp.exp(z - sp - later - r_scr[...])
        if mask is not None:
            wgt = jnp.where(mask, wgt, 0.0)
        r_scr[...] = r_scr[...] + jnp.sum(sp, axis=-1, keepdims=True)
        as_scr[...] = as_scr[...] + weighted(wgt, vts)

    @pl.when(p == 0)
    def _():
        m_scr[...] = jnp.full_like(m_scr, -jnp.inf)
        l_scr[...] = jnp.zeros_like(l_scr)
        af_scr[...] = jnp.zeros_like(af_scr)
        r_scr[...] = jnp.zeros_like(r_scr)
        as_scr[...] = jnp.zeros_like(as_scr)
        c_scr[...] = jnp.zeros_like(c_scr)
        fox_update([kn_f_ref[0]], [vn_f_ref[0]], -frow_ref[0], key <= tok)
        sb_update([kn_s_ref[0]], [vn_s_ref[0]], key < tok)

    @pl.when(p > 0)
    def _():
        lp = jnp.concatenate([pages[5 * j + 2][0] for j in range(gp)], axis=1)
        sfx = suffix(gp * ps)
        hi, mid, lo = _split3(lp)
        suf = (jnp.dot(hi, sfx, preferred_element_type=F32) + jnp.dot(mid, sfx, preferred_element_type=F32)
               + jnp.dot(lo, sfx, preferred_element_type=F32)) + c_scr[...]
        c_scr[...] = c_scr[...] + jnp.sum(lp, axis=-1, keepdims=True)
        fexp = jnp.concatenate([jnp.broadcast_to(suf[h:h + 1], (nt, gp * ps)) for h in range(nh)], axis=0)
        fox_update([pages[5 * j][0] for j in range(gp)], [pages[5 * j + 1][0] for j in range(gp)], fexp, None)
        sb_update([pages[5 * j + 3][0] for j in range(gp)], [pages[5 * j + 4][0] for j in range(gp)], None)

    @pl.when(p == pl.num_programs(1) - 1)
    def _():
        def extract(acc):
            out = jnp.zeros((nt, w), F32)
            for h in range(nh):
                out = out + jnp.where(own[h * nt:(h + 1) * nt], acc[h * nt:(h + 1) * nt], 0.0)
            return out
        fo_ref[...] = extract(af_scr[...] / l_scr[...]).astype(fo_ref.dtype)
        so_ref[...] = extract(as_scr[...]).astype(so_ref.dtype)


def _attn_sample(fqt, fkt, fvt, fct, sqt, skt, svt, b_sb, pools, page_table, nb, nt, nh):
    pool_fk, pool_fv, pool_lf, pool_sk, pool_sv = pools
    w, n = fqt.shape
    hd = w // nh
    nphys, ps = pool_fk.shape[:2]
    npg = page_table.shape[1]
    rows = nh * nt
    gp = next(g for g in (4, 2, 1) if npg % g == 0)
    ngrp = npg // gp
    assert (nt & (nt - 1)) == 0 and (hd & (hd - 1)) == 0 and nt <= ps

    def newpage(at):
        return jnp.pad(at.astype(F32).reshape(w, nb, nt).transpose(1, 0, 2), ((0, 0), (0, 0), (0, ps - nt)))

    fnew = fct.reshape(nh, nb, nt).transpose(1, 0, 2)
    frow = jnp.pad(jnp.repeat(fnew, nt, axis=1), ((0, 0), (0, 0), (0, ps - nt)))
    bcol = jnp.repeat(b_sb.astype(F32), nt).reshape(rows, 1)
    native = lambda a: a.transpose(0, 2, 3, 1).reshape(nphys, w, ps)
    lpt = pool_lf.astype(F32).transpose(0, 2, 1)
    pt = page_table.reshape(-1).astype(jnp.int32)

    qspec = pl.BlockSpec((nt, w), lambda b, p, pt: (b, 0))
    nspec = pl.BlockSpec((1, w, ps), lambda b, p, pt: (b, 0, 0))

    def page(j):
        return lambda b, p, pt: (pt[b * npg + (ngrp - jnp.maximum(p, 1)) * gp + j], 0, 0)

    page_specs, page_args = [], []
    for j in range(gp):
        page_specs += [pl.BlockSpec((1, w, ps), page(j)), pl.BlockSpec((1, w, ps), page(j)),
                       pl.BlockSpec((1, nh, ps), page(j)),
                       pl.BlockSpec((1, w, ps), page(j)), pl.BlockSpec((1, w, ps), page(j))]
        page_args += [native(pool_fk), native(pool_fv), lpt, native(pool_sk), native(pool_sv)]
    return pl.pallas_call(
        functools.partial(_attn_sample_kernel, nh=nh, hd=hd, nt=nt, ps=ps, gp=gp),
        grid_spec=pltpu.PrefetchScalarGridSpec(
            num_scalar_prefetch=1, grid=(nb, ngrp + 1),
            in_specs=[qspec, nspec, nspec, qspec, nspec, nspec,
                      pl.BlockSpec((1, rows, ps), lambda b, p, pt: (b, 0, 0)),
                      pl.BlockSpec((rows, 1), lambda b, p, pt: (0, 0))] + page_specs,
            out_specs=[qspec, qspec],
            scratch_shapes=[pltpu.VMEM((rows, 1), F32), pltpu.VMEM((rows, 1), F32), pltpu.VMEM((rows, w), F32),
                            pltpu.VMEM((rows, 1), F32), pltpu.VMEM((rows, w), F32), pltpu.VMEM((nh, 1), F32)]),
        out_shape=[jax.ShapeDtypeStruct((n, w), F32), jax.ShapeDtypeStruct((n, w), F32)],
        compiler_params=_cparams("parallel", "arbitrary"),
        name="attn_sample",
    )(pt, fqt.astype(F32).T, newpage(fkt), newpage(fvt), sqt.astype(F32).T, newpage(skt), newpage(svt),
      frow, bcol, *page_args)


def _trunk(x3, mod, ab_state, past, page_table, wts, t_attn):
    nb, seq, d = x3.shape
    n = nb * seq
    x = x3.reshape(n, d)
    tm = _tile(n, 256)
    if seq % tm == 0:
        modg = [mod[l][:, None, :] for l in range(2)]
        tpb = seq // tm
    else:
        assert tm % seq == 0
        modg = [jnp.repeat(mod[l], seq, axis=0).reshape(n // tm, tm, -1) for l in range(2)]
        tpb = 1
    nh = wts['nh_att']

    x = _ffn(x, modg[0], 0, tpb, tm, wts['ffn_in'][0][0], wts['ffn_out'][0][0])
    proj = _proj(x, modg[0], 3, tpb, tm, wts['ab_w_in'])
    w5 = wts['s5_glu_w'].shape[0]
    ys, s5_re, s5_im = _s5(proj[:, :w5], ab_state[0], ab_state[1], wts['s5'], nb, seq)
    oh, hg = _hgrn(proj, w5, wts['hgrn_lb'], wts['hgrn_norm_w'], ab_state[2], nb, seq)
    x = _ab_out(x, modg[0], 5, tpb, tm, ys, oh, wts['s5_glu_w'], wts['s5_glu_b'], wts['ab_w_out'])
    x = _ffn(x, modg[0], 6, tpb, tm, wts['ffn_in'][0][1], wts['ffn_out'][0][1])

    x = _ffn(x, modg[1], 0, tpb, tm, wts['ffn_in'][1][0], wts['ffn_out'][1][0])
    (fkt, fvt, skt, svt, fqb, fkb, fvb, sqb, skb, svb, lft, fct) = _cd_in(
        x, modg[1], 3, tpb, tm, wts['cd_wt'], wts['cd_wk'], wts['cd_wf'], wts['cd_bf'], nb, seq)
    if past is None:
        fo, so = _attn_prompt(fqb, fkb, fvb, fct, sqb, skb, svb, wts['cd_b_sb'], nb, seq, nh, t_attn)
    else:
        fo, so = _attn_sample(fqb.T, fkt[0], fvt[0], fct[0], sqb.T, skt[0], svt[0], wts['cd_b_sb'],
                              past, page_table, nb, seq, nh)
    x = _cd_out(x, modg[1], 5, tpb, tm, fo, so, wts['cd_w_out'])
    y = _ffn(x, modg[1], 6, tpb, tm, wts['ffn_in'][1][1], wts['ffn_out'][1][1], wts['final_norm_w'])

    hd = fkt.shape[1] // nh
    if fkt.shape[0] == nb:
        kv = lambda a: a.reshape(nb, nh, hd, seq).transpose(0, 3, 1, 2)
        logf = lft.transpose(0, 2, 1)
    else:
        kv = lambda a: a[0].T.reshape(nb, seq, nh, hd)
        logf = lft[0].T.reshape(nb, seq, nh)
    return (y.reshape(nb, seq, d),
            (s5_re, s5_im, hg, kv(fkt), kv(fvt), logf, kv(skt), kv(svt)))


def kernel(x_prompt, x_sample, state_s5_re, state_s5_im, state_hgrn, cache_fox_k, cache_fox_v,
           cache_fox_logf, cache_sb_k, cache_sb_v, page_table, c_prompt, c_sample, ada_w, ada_b,
           ffn_w_in, ffn_w_out, ab_w_in, ab_w_out, s5_a_re, s5_a_im, s5_log_dt, s5_b_re, s5_b_im,
           s5_c_re, s5_c_im, s5_d, s5_glu_w, s5_glu_b, hgrn_lb_logits, hgrn_norm_w, cd_w_in, cd_b_f,
           cd_b_sb, cd_w_out, final_norm_w, attn_tile=512):
    bp, seq_p, d = x_prompt.shape
    bs = x_sample.shape[0]
    nh_att, hd = cache_fox_k.shape[2:]
    wa = nh_att * hd
    g, p = s5_a_re.shape
    nhh, dk, dv = state_hgrn.shape[1:]

    nc_all = bp + bs
    c_all = jnp.pad(jnp.concatenate([c_prompt, c_sample], axis=0), ((0, (-nc_all) % 8), (0, 0)))
    mod = _ada(c_all, ada_w, ada_b)
    mod_p = [mod[l, :bp] for l in range(2)]
    mod_s = [mod[l, bp:nc_all] for l in range(2)]

    lb = jnp.cumsum(jax.nn.softmax(hgrn_lb_logits.astype(F32), axis=0), axis=0)[0].reshape(1, nhh * dk)

    wts = dict(
        nh_att=nh_att,
        ffn_in=[[ffn_w_in[l, s].astype(BF16) for s in range(2)] for l in range(2)],
        ffn_out=[[ffn_w_out[l, s].astype(BF16) for s in range(2)] for l in range(2)],
        ab_w_in=ab_w_in.astype(BF16), ab_w_out=ab_w_out.astype(BF16),
        s5=(s5_a_re, s5_a_im, s5_log_dt, s5_b_re, s5_b_im, s5_c_re, s5_c_im, s5_d),
        s5_glu_w=s5_glu_w.astype(BF16), s5_glu_b=s5_glu_b, hgrn_lb=lb, hgrn_norm_w=hgrn_norm_w,
        cd_wt=jnp.concatenate([cd_w_in[:, wa:3 * wa], cd_w_in[:, 4 * wa + nh_att:]], axis=1).T.astype(BF16),
        cd_wk=jnp.concatenate([cd_w_in[:, :3 * wa], cd_w_in[:, 3 * wa + nh_att:]], axis=1).astype(BF16),
        cd_wf=cd_w_in[:, 3 * wa:3 * wa + nh_att].T.astype(BF16),
        cd_bf=cd_b_f.astype(F32).reshape(nh_att, 1),
        cd_b_sb=cd_b_sb, cd_w_out=cd_w_out.astype(BF16), final_norm_w=final_norm_w)

    zeros = lambda *s: jnp.zeros(s, F32)
    y_p, st_p = _trunk(x_prompt, mod_p, (zeros(bp, g, p), zeros(bp, g, p), zeros(bp, nhh, dk, dv)),
                       None, None, wts, _tile(seq_p, attn_tile))
    y_s, st_s = _trunk(x_sample, mod_s, (state_s5_re, state_s5_im, state_hgrn),
                       (cache_fox_k, cache_fox_v, cache_fox_logf, cache_sb_k, cache_sb_v),
                       page_table, wts, None)
    return (y_p, y_s) + st_p + st_s
```

```python
import functools
import math

import jax
import jax.numpy as jnp
from jax import lax
from jax.experimental import pallas as pl
from jax.experimental.pallas import tpu as pltpu

F32 = jnp.float32
BF16 = jnp.bfloat16
EPS = 1e-6
N_ADA = 9
S5_GROUP_CH = 16
S5_CHUNK = 64
HGRN_CHUNK = 64
SB_SUB = 256
ATT_ROWS = 64
ATT_TILE = 512
LANES = 128
VMEM_LIMIT = 56 * 1024 * 1024


def _cparams(*sem):
    return pltpu.CompilerParams(dimension_semantics=sem, vmem_limit_bytes=VMEM_LIMIT)


def _tile(n, pref):
    if n <= pref:
        return n
    t = pref - pref % 8
    while n % t:
        t -= 8
    return t


def _bdot(a, b):
    return jnp.dot(a.astype(BF16), b.astype(BF16), preferred_element_type=F32)


def _bdot_nt(a, b):
    return lax.dot_general(a.astype(BF16), b.astype(BF16), (((1,), (1,)), ((), ())),
                           preferred_element_type=F32)


def _bdot_tn(a, b):
    return lax.dot_general(a.astype(BF16), b.astype(BF16), (((0,), (0,)), ((), ())),
                           preferred_element_type=F32)


def _split2(x):
    hi = x.astype(BF16)
    lo = (x - hi.astype(F32)).astype(BF16)
    return hi, lo


def _split3(x):
    hi = x.astype(BF16)
    r = x - hi.astype(F32)
    mid = r.astype(BF16)
    lo = (r - mid.astype(F32)).astype(BF16)
    return hi, mid, lo


def _sigmoid(x):
    return 1.0 / (1.0 + jnp.exp(-x))


def _silu(x):
    return x * _sigmoid(x)


def _softplus(x):
    return jnp.maximum(x, 0.0) + jnp.log(1.0 + jnp.exp(-jnp.abs(x)))


def _rms(x):
    return x * lax.rsqrt(jnp.mean(x * x, axis=-1, keepdims=True) + EPS)


def _modulate(x, shift, scale):
    return _rms(x) * (1.0 + scale) + shift


def _mod_spec(r, j, tpb, d):
    return pl.BlockSpec((1, r, d), lambda i: (i // tpb, 0, j))


def _const_spec(shape):
    nd = len(shape)
    return pl.BlockSpec(shape, lambda *_: (0,) * nd, pipeline_mode=pl.Buffered(1))


def _ada_kernel(c_ref, w_ref, b_ref, o_ref):
    s = _silu(c_ref[...])
    w = w_ref[0]
    sh, sl = _split2(s)
    wh, wl = _split2(w)
    acc = jnp.dot(sh, wh, preferred_element_type=F32)
    acc += jnp.dot(sh, wl, preferred_element_type=F32)
    acc += jnp.dot(sl, wh, preferred_element_type=F32)
    o_ref[0] = acc + b_ref[0]


def _ada(c_all, ada_w, ada_b):
    depth, d, nout = ada_w.shape
    m = c_all.shape[0]
    tn = _tile(nout, 1152) if nout % LANES == 0 else nout
    return pl.pallas_call(
        _ada_kernel,
        grid=(depth, nout // tn),
        in_specs=[pl.BlockSpec((m, d), lambda l, j: (0, 0)),
                  pl.BlockSpec((1, d, tn), lambda l, j: (l, 0, j)),
                  pl.BlockSpec((1, 1, tn), lambda l, j: (l, 0, j))],
        out_specs=pl.BlockSpec((1, m, tn), lambda l, j: (l, 0, j)),
        out_shape=jax.ShapeDtypeStruct((depth, m, nout), F32),
        compiler_params=_cparams("arbitrary", "arbitrary"),
        name="ada",
    )(c_all, ada_w, ada_b.reshape(depth, 1, nout))


def _ffn_kernel(*refs, dff, final):
    if final:
        x_ref, sh_ref, sc_ref, gt_ref, win_ref, wout_ref, fw_ref, o_ref = refs
    else:
        x_ref, sh_ref, sc_ref, gt_ref, win_ref, wout_ref, o_ref = refs
    x = x_ref[...]
    h = _modulate(x, sh_ref[0], sc_ref[0]).astype(BF16)
    gu = jnp.dot(h, win_ref[...], preferred_element_type=F32)
    g = gu[:, :dff]
    u = gu[:, dff:]
    a = (_silu(g) * u).astype(BF16)
    out = jnp.dot(a, wout_ref[...], preferred_element_type=F32)
    y = x + 0.5 * gt_ref[0] * out
    if final:
        y = _rms(y) * fw_ref[...]
    o_ref[...] = y


def _ffn(x, modg, j0, tpb, tm, w_in, w_out, final_w=None):
    n, d = x.shape
    r = modg.shape[1]
    dff = w_out.shape[0]
    final = final_w is not None
    in_specs = [pl.BlockSpec((tm, d), lambda i: (i, 0)),
                _mod_spec(r, j0, tpb, d), _mod_spec(r, j0 + 1, tpb, d), _mod_spec(r, j0 + 2, tpb, d),
                _const_spec(w_in.shape), _const_spec(w_out.shape)]
    args = [x, modg, modg, modg, w_in, w_out]
    if final:
        in_specs.append(_const_spec((1, d)))
        args.append(final_w.reshape(1, d))
    return pl.pallas_call(
        functools.partial(_ffn_kernel, dff=dff, final=final),
        grid=(n // tm,),
        in_specs=in_specs,
        out_specs=pl.BlockSpec((tm, d), lambda i: (i, 0)),
        out_shape=jax.ShapeDtypeStruct((n, d), F32),
        compiler_params=_cparams("parallel"),
        name="ffn",
    )(*args)


def _proj_kernel(x_ref, sh_ref, sc_ref, w_ref, o_ref):
    h = _modulate(x_ref[...], sh_ref[0], sc_ref[0]).astype(BF16)
    o_ref[...] = jnp.dot(h, w_ref[...], preferred_element_type=F32)


def _proj(x, modg, j0, tpb, tm, w):
    n, d = x.shape
    r = modg.shape[1]
    nout = w.shape[1]
    return pl.pallas_call(
        _proj_kernel,
        grid=(n // tm,),
        in_specs=[pl.BlockSpec((tm, d), lambda i: (i, 0)),
                  _mod_spec(r, j0, tpb, d), _mod_spec(r, j0 + 1, tpb, d),
                  _const_spec(w.shape)],
        out_specs=pl.BlockSpec((tm, nout), lambda i: (i, 0)),
        out_shape=jax.ShapeDtypeStruct((n, nout), F32),
        compiler_params=_cparams("parallel"),
        name="ab_in",
    )(x, modg, modg, w)


def _cmul_rows(x, a1, a2):
    return x * a1 + pltpu.roll(x, x.shape[-1] // 2, 1) * a2


def _s5_kernel(u_ref, tb_ref, sm_ref, ym_ref, a1_ref, a2_ref, h0_ref, y_ref, hf_ref, tm_scr, *, nb, nc):
    nblk = tb_ref.shape[1]

    @pl.when(pl.program_id(0) == 0)
    def _():
        tm_scr[...] = jnp.zeros_like(tm_scr)

    for s in range(nblk):
        for t in range(s, nblk):
            tm_scr[s * LANES:(s + 1) * LANES, t * LANES:(t + 1) * LANES] = tb_ref[0, t - s]
    u = u_ref[0].astype(BF16)
    e = jnp.dot(u, sm_ref[0], preferred_element_type=F32)
    a1 = a1_ref[0]
    a2 = a2_ref[0]
    h0 = h0_ref[0]
    if nc == 1:
        hprev = h0
        hf_ref[0] = _cmul_rows(h0, a1[0:1], a2[0:1]) + e
    else:
        row = lax.broadcasted_iota(jnp.int32, (nc, e.shape[1]), 0)
        prevs = []
        for b in range(nb):
            h0b = h0[b:b + 1]
            x = e[b * nc:(b + 1) * nc]
            x = x + jnp.where(row == 0, _cmul_rows(h0b, a1[0:1], a2[0:1]), 0.0)
            d, k = 1, 0
            while d < nc:
                sh = jnp.where(row >= d, pltpu.roll(x, d, 0), 0.0)
                x = x + _cmul_rows(sh, a1[k:k + 1], a2[k:k + 1])
                d, k = 2 * d, k + 1
            hf_ref[0, b:b + 1, :] = x[nc - 1:nc]
            prevs.append(jnp.where(row == 0, h0b, pltpu.roll(x, 1, 0)))
        hprev = jnp.concatenate(prevs, axis=0)
    y_ref[0] = (jnp.dot(u, tm_scr[...], preferred_element_type=F32)
                + _bdot(hprev, ym_ref[0]))


def _s5_params(a_re, a_im, log_dt, b_re, b_im, c_re, c_im, d_skip, t, nc):
    hp = lax.Precision.HIGHEST
    g, p = a_re.shape
    hc = b_re.shape[2]
    dt = jnp.exp(log_dt.astype(F32))[:, None]
    lam_re, lam_im = a_re.astype(F32), a_im.astype(F32)
    mag = jnp.exp(lam_re * dt)
    lbar_re = mag * jnp.cos(lam_im * dt)
    lbar_im = mag * jnp.sin(lam_im * dt)
    den = lam_re * lam_re + lam_im * lam_im
    nr = lbar_re - 1.0
    zr = (nr * lam_re + lbar_im * lam_im) / den
    zi = (lbar_im * lam_re - nr * lam_im) / den
    bb_re = zr[..., None] * b_re - zi[..., None] * b_im
    bb_im = zr[..., None] * b_im + zi[..., None] * b_re

    def powers(exps):
        ef = jnp.asarray(exps, F32)[:, None, None]
        m = jnp.exp(lam_re * dt * ef)
        return m * jnp.cos(lam_im * dt * ef), m * jnp.sin(lam_im * dt * ef)

    pr, pi = powers(list(range(t + 1)))
    pb_re = pr[:t, :, :, None] * bb_re - pi[:t, :, :, None] * bb_im
    pb_im = pr[:t, :, :, None] * bb_im + pi[:t, :, :, None] * bb_re
    kk = (jnp.einsum('gop,tgpi->gtio', c_re, pb_re, precision=hp)
          - jnp.einsum('gop,tgpi->gtio', c_im, pb_im, precision=hp))
    kk = kk.at[:, 0].add(d_skip[:, :, None] * jnp.eye(hc, dtype=F32))
    sb = LANES // hc
    nblk = t // sb
    lag = (sb * jnp.arange(nblk)[:, None, None] + jnp.arange(sb)[None, None, :]
           - jnp.arange(sb)[None, :, None])
    tb = jnp.where((lag >= 0)[None, :, :, :, None, None], kk[:, jnp.clip(lag, 0, t - 1)], 0.0)
    tb = tb.transpose(0, 1, 2, 4, 3, 5).reshape(g, nblk, LANES, LANES)
    sm = jnp.concatenate([pb_re[::-1].transpose(1, 0, 3, 2), pb_im[::-1].transpose(1, 0, 3, 2)], axis=-1)
    sm = sm.reshape(g, t * hc, 2 * p)
    p1r, p1i = pr[1:], pi[1:]
    y_r = c_re[None] * p1r[:, :, None, :] - c_im[None] * p1i[:, :, None, :]
    y_i = -c_re[None] * p1i[:, :, None, :] - c_im[None] * p1r[:, :, None, :]
    ym = jnp.concatenate([y_r, y_i], axis=-1).transpose(1, 3, 0, 2).reshape(g, 2 * p, t * hc)
    nk = max(1, int(math.log2(nc))) if nc > 1 else 1
    ar, ai = powers([t * (2 ** k) for k in range(nk)])
    a1 = jnp.concatenate([ar, ar], axis=-1).transpose(1, 0, 2)
    a2 = jnp.concatenate([-ai, ai], axis=-1).transpose(1, 0, 2)
    return tb.astype(BF16), sm.astype(BF16), ym.astype(BF16), a1, a2


def _s5(u, h0_re, h0_im, s5w, nb, seq):
    a_re, a_im, log_dt, b_re, b_im, c_re, c_im, d_skip = s5w
    g, p = a_re.shape
    hc = b_re.shape[2]
    t = min(S5_CHUNK, seq)
    nc = seq // t
    m = nb * nc
    tb, sm, ym, a1, a2 = _s5_params(a_re, a_im, log_dt, b_re, b_im, c_re, c_im, d_skip, t, nc)
    nk = a1.shape[1]
    nblk = tb.shape[1]
    uf = u.reshape(nb, nc, t, g, hc).transpose(3, 0, 1, 2, 4).reshape(g, m, t * hc)
    h0 = jnp.concatenate([h0_re, h0_im], axis=-1).transpose(1, 0, 2)
    y, hf = pl.pallas_call(
        functools.partial(_s5_kernel, nb=nb, nc=nc),
        grid=(g,),
        in_specs=[pl.BlockSpec((1, m, t * hc), lambda i: (i, 0, 0)),
                  pl.BlockSpec((1, nblk, LANES, LANES), lambda i: (i, 0, 0, 0)),
                  pl.BlockSpec((1, t * hc, 2 * p), lambda i: (i, 0, 0)),
                  pl.BlockSpec((1, 2 * p, t * hc), lambda i: (i, 0, 0)),
                  pl.BlockSpec((1, nk, 2 * p), lambda i: (i, 0, 0)),
                  pl.BlockSpec((1, nk, 2 * p), lambda i: (i, 0, 0)),
                  pl.BlockSpec((1, nb, 2 * p), lambda i: (i, 0, 0))],
        out_specs=[pl.BlockSpec((1, m, t * hc), lambda i: (i, 0, 0)),
                   pl.BlockSpec((1, nb, 2 * p), lambda i: (i, 0, 0))],
        out_shape=[jax.ShapeDtypeStruct((g, m, t * hc), F32),
                   jax.ShapeDtypeStruct((g, nb, 2 * p), F32)],
        scratch_shapes=[pltpu.VMEM((t * hc, t * hc), BF16)],
        compiler_params=_cparams("arbitrary"),
        name="s5",
    )(uf, tb, sm, ym, a1, a2, h0)
    y = y.reshape(g, nb, nc, t, hc).transpose(1, 2, 3, 0, 4).reshape(nb * seq, g * hc)
    hf = hf.transpose(1, 0, 2)
    return y, hf[..., :p], hf[..., p:]


def _hgrn_kernel(q_ref, f_ref, i_ref, g_ref, lb_ref, nw_ref, s0_ref, o_ref, sf_ref, st_scr, *, c, nch, nh, dk, bb):
    li = pl.program_id(1)

    @pl.when(li == 0)
    def _():
        st_scr[...] = s0_ref[...]

    nw = nw_ref[...]
    row = lax.broadcasted_iota(jnp.int32, (c, c), 0)
    col = lax.broadcasted_iota(jnp.int32, (c, c), 1)
    causal = col <= row
    tril = causal.astype(BF16)
    mid = max(c // 2 - 1, 0)

    def chunk(ci, carry):
        sl = pl.ds(pl.multiple_of(ci * c, c), c)
        for bi in range(bb):
            for h in range(nh):
                hs = slice(h * dk, (h + 1) * dk)
                lb = lb_ref[:, hs]
                f = lb + (1.0 - lb) * _sigmoid(f_ref[bi, sl, hs])
                k = 1.0 - f
                q = _silu(q_ref[bi, sl, hs])
                v = i_ref[bi, sl, hs]
                gh, gl = _split2(jnp.log(f))
                b = (jnp.dot(tril, gh, preferred_element_type=F32)
                     + jnp.dot(tril, gl, preferred_element_type=F32))
                r = b[mid:mid + 1]
                sc = jnp.where(causal, _bdot_nt(q * jnp.exp(b - r), k * jnp.exp(r - b)), 0.0)
                st = st_scr[bi, h]
                o = _bdot_nt(q * jnp.exp(b), st) + _bdot(sc, v)
                bl = b[c - 1:c]
                st_scr[bi, h] = st * jnp.exp(bl) + _bdot_tn(v, k * jnp.exp(bl - b))
                o_ref[bi, sl, hs] = _rms(o) * nw * _silu(g_ref[bi, sl, hs])
        return carry

    lax.fori_loop(0, nch, chunk, 0)

    @pl.when(li == pl.num_programs(1) - 1)
    def _():
        sf_ref[...] = st_scr[...]


def _hgrn(proj, col0, lb, norm_w, s0, nb, seq):
    nh, dk, dv = s0.shape[1:]
    n = nb * seq
    c = min(HGRN_CHUNK, seq)
    tl = _tile(seq, 512)
    nl = seq // tl
    bb = 2 if nb % 2 == 0 else 1
    s0t = s0.astype(F32).transpose(0, 1, 3, 2)
    wh = nh * dk
    assert dk == dv and col0 % wh == 0
    cb = col0 // wh
    p3 = proj.reshape(nb, seq, proj.shape[1])

    def colspec(j):
        return pl.BlockSpec((bb, tl, wh), lambda b, l: (b, l, cb + j))

    o, sft = pl.pallas_call(
        functools.partial(_hgrn_kernel, c=c, nch=tl // c, nh=nh, dk=dk, bb=bb),
        grid=(nb // bb, nl),
        in_specs=[colspec(0), colspec(1), colspec(2), colspec(3),
                  pl.BlockSpec((1, wh), lambda b, l: (0, 0)),
                  pl.BlockSpec((1, dv), lambda b, l: (0, 0)),
                  pl.BlockSpec((bb, nh, dv, dk), lambda b, l: (b, 0, 0, 0))],
        out_specs=[pl.BlockSpec((bb, tl, wh), lambda b, l: (b, l, 0)),
                   pl.BlockSpec((bb, nh, dv, dk), lambda b, l: (b, 0, 0, 0))],
        out_shape=[jax.ShapeDtypeStruct((nb, seq, nh * dv), F32),
                   jax.ShapeDtypeStruct((nb, nh, dv, dk), F32)],
        scratch_shapes=[pltpu.VMEM((bb, nh, dv, dk), F32)],
        compiler_params=_cparams("parallel", "arbitrary"),
        name="hgrn",
    )(p3, p3, p3, p3, lb, norm_w.reshape(1, dv), s0t)
    return o.reshape(n, nh * dv), sft.transpose(0, 1, 3, 2)


def _gelu_tanh(x):
    return x * (0.5 * (1.0 + jnp.tanh(math.sqrt(2.0 / math.pi) * (x + 0.044715 * (x * x * x)))))


def _about_kernel(x_ref, gt_ref, ys_ref, oh_ref, gw_ref, gb_ref, wo_ref, o_ref, *, w5):
    y = _gelu_tanh(ys_ref[...])
    s5 = y * _sigmoid(_bdot(y, gw_ref[...]) + gb_ref[...])
    out = _bdot(s5, wo_ref[:w5, :]) + _bdot(oh_ref[...], wo_ref[w5:, :])
    o_ref[...] = x_ref[...] + gt_ref[0] * out


def _ab_out(x, modg, j, tpb, tm, ys, oh, glu_w, glu_b, w_out):
    n, d = x.shape
    r = modg.shape[1]
    w5 = ys.shape[1]
    wh = oh.shape[1]
    return pl.pallas_call(
        functools.partial(_about_kernel, w5=w5),
        grid=(n // tm,),
        in_specs=[pl.BlockSpec((tm, d), lambda i: (i, 0)), _mod_spec(r, j, tpb, d),
                  pl.BlockSpec((tm, w5), lambda i: (i, 0)), pl.BlockSpec((tm, wh), lambda i: (i, 0)),
                  _const_spec(glu_w.shape), _const_spec((1, w5)), _const_spec(w_out.shape)],
        out_specs=pl.BlockSpec((tm, d), lambda i: (i, 0)),
        out_shape=jax.ShapeDtypeStruct((n, d), F32),
        compiler_params=_cparams("parallel"),
        name="ab_out",
    )(x, modg, ys, oh, glu_w, glu_b.reshape(1, w5), w_out)


def _cdout_kernel(x_ref, gt_ref, fo_ref, so_ref, wo_ref, o_ref, *, wf):
    out = _bdot(fo_ref[...], wo_ref[:wf, :]) + _bdot(so_ref[...], wo_ref[wf:, :])
    o_ref[...] = x_ref[...] + gt_ref[0] * out


def _cd_out(x, modg, j, tpb, tm, fo, so, w_out):
    n, d = x.shape
    r = modg.shape[1]
    wf = fo.shape[1]
    ws = so.shape[1]
    return pl.pallas_call(
        functools.partial(_cdout_kernel, wf=wf),
        grid=(n // tm,),
        in_specs=[pl.BlockSpec((tm, d), lambda i: (i, 0)), _mod_spec(r, j, tpb, d),
                  pl.BlockSpec((tm, wf), lambda i: (i, 0)), pl.BlockSpec((tm, ws), lambda i: (i, 0)),
                  _const_spec(w_out.shape)],
        out_specs=pl.BlockSpec((tm, d), lambda i: (i, 0)),
        out_shape=jax.ShapeDtypeStruct((n, d), F32),
        compiler_params=_cparams("parallel"),
        name="cd_out",
    )(x, modg, fo, so, w_out)


def _cdin_kernel(x_ref, sh_ref, sc_ref, wt_ref, wk_ref, wf_ref, bf_ref,
                 fkt_ref, fvt_ref, skt_ref, svt_ref, fqb_ref, fkb_ref, fvb_ref, sqb_ref, skb_ref, svb_ref,
                 lft_ref, fct_ref, carry_scr, *, w, seg, tps, qscale):
    i = pl.program_id(0)
    tm = x_ref.shape[0]
    h = _modulate(x_ref[...], sh_ref[0], sc_ref[0]).astype(BF16)
    pt = _bdot_nt(wt_ref[...], h)
    fkt_ref[0] = pt[0:w]
    fvt_ref[0] = pt[w:2 * w]
    skt_ref[0] = pt[2 * w:3 * w]
    svt_ref[0] = pt[3 * w:4 * w]
    pk = jnp.dot(h, wk_ref[...], preferred_element_type=F32)
    fqb_ref[...] = (pk[:, 0:w] * qscale).astype(BF16)
    fkb_ref[...] = pk[:, w:2 * w].astype(BF16)
    fvb_ref[...] = pk[:, 2 * w:3 * w].astype(BF16)
    sqb_ref[...] = (pk[:, 3 * w:4 * w] * qscale).astype(BF16)
    skb_ref[...] = pk[:, 4 * w:5 * w].astype(BF16)
    svb_ref[...] = pk[:, 5 * w:6 * w].astype(BF16)
    flt = _bdot_nt(wf_ref[...], h) + bf_ref[...]
    lft = -_softplus(-flt)
    lft_ref[0] = lft
    row = lax.broadcasted_iota(jnp.int32, (tm, tm), 0)
    col = lax.broadcasted_iota(jnp.int32, (tm, tm), 1)
    msk = row <= col
    if seg < tm:
        sh = int(math.log2(seg))
        msk = msk & (lax.shift_right_logical(col, sh) == lax.shift_right_logical(row, sh))
    mb = msk.astype(BF16)
    hi, mid, lo = _split3(lft)
    cum = (jnp.dot(hi, mb, preferred_element_type=F32) + jnp.dot(mid, mb, preferred_element_type=F32)
           + jnp.dot(lo, mb, preferred_element_type=F32))
    if tps > 1:
        @pl.when(i % tps == 0)
        def _():
            carry_scr[...] = jnp.zeros_like(carry_scr)
        cum = cum + carry_scr[...]
        carry_scr[...] = cum[:, tm - 1:tm]
    fct_ref[0] = cum


def _cd_in(x, modg, j0, tpb, tm, wt, wk, wf, bf, nb, seq):
    n, d = x.shape
    r = modg.shape[1]
    w = wt.shape[0] // 4
    nh = wf.shape[0]
    hd = w // nh
    seg = min(seq, tm)
    assert seg == tm or (seg & (seg - 1)) == 0
    tps = max(seq // tm, 1)
    nbk, lk = (nb, seq) if seq >= tm else (1, n)
    tpk = lk // tm
    tshape = lambda rows, dt: jax.ShapeDtypeStruct((nbk, rows, lk), dt)
    tspec = lambda rows: pl.BlockSpec((1, rows, tm), lambda i: (i // tpk, 0, i % tpk))
    kspec = pl.BlockSpec((tm, w), lambda i: (i, 0))
    return pl.pallas_call(
        functools.partial(_cdin_kernel, w=w, seg=seg, tps=tps, qscale=hd ** -0.5),
        grid=(n // tm,),
        in_specs=[pl.BlockSpec((tm, d), lambda i: (i, 0)),
                  _mod_spec(r, j0, tpb, d), _mod_spec(r, j0 + 1, tpb, d),
                  _const_spec(wt.shape), _const_spec(wk.shape), _const_spec(wf.shape), _const_spec(bf.shape)],
        out_specs=[tspec(w)] * 4 + [kspec] * 6 + [tspec(nh), tspec(nh)],
        out_shape=[tshape(w, F32)] * 4 + [jax.ShapeDtypeStruct((n, w), BF16)] * 6
                  + [tshape(nh, F32), tshape(nh, F32)],
        scratch_shapes=[pltpu.VMEM((nh, 1), F32)],
        compiler_params=_cparams("arbitrary"),
        name="cd_in",
    )(x, modg, modg, wt, wk, wf, bf)


def _head_lanes(q, hd):
    lane = lax.broadcasted_iota(jnp.int32, (1, 2 * hd), 1)
    return [jnp.where((lane >= hh * hd) & (lane < (hh + 1) * hd), q, jnp.zeros_like(q)) for hh in range(2)]


def _fox_kernel(q_ref, k_ref, v_ref, fk_ref, o_ref, *, t, hd):
    qi = pl.program_id(2)
    qm = _head_lanes(q_ref[...], hd)
    lane = lax.broadcasted_iota(jnp.int32, (1, 2 * hd), 1)
    row = lax.broadcasted_iota(jnp.int32, (t, t), 0)
    col = lax.broadcasted_iota(jnp.int32, (t, t), 1)

    def block(ki, carry, masked):
        ks = pl.multiple_of(ki * t, t)
        k = k_ref[pl.ds(ks, t), :]
        v = v_ref[pl.ds(ks, t), :]
        new = []
        for hh in range(2):
            m, l, acc = carry[hh]
            s = _bdot_nt(qm[hh], k) - fk_ref[0, 0, hh:hh + 1, pl.ds(ks, t)]
            if masked:
                s = jnp.where(col <= row, s, -jnp.inf)
            mn = jnp.maximum(m, jnp.max(s, axis=-1, keepdims=True))
            alpha = jnp.exp(m - mn)
            p = jnp.exp(s - mn)
            l = alpha * l + jnp.sum(p, axis=-1, keepdims=True)
            acc = alpha * acc + jnp.dot(p.astype(BF16), v, preferred_element_type=F32)
            new.append((mn, l, acc))
        return tuple(new)

    init = tuple((jnp.full((t, 1), -jnp.inf, F32), jnp.zeros((t, 1), F32), jnp.zeros((t, 2 * hd), F32))
                 for _ in range(2))
    carry = lax.fori_loop(0, qi, functools.partial(block, masked=False), init)
    carry = block(qi, carry, True)
    outs = [acc / l for (_, l, acc) in carry]
    o_ref[...] = jnp.where(lane < hd, outs[0], outs[1]).astype(o_ref.dtype)


def _sb_kernel(bsb_ref, q_ref, k_ref, v_ref, o_ref, s_scr, sp_scr, lt_scr, p_scr, r_scr, acc_scr, *, t, hd):
    pair = pl.program_id(1)
    qi = pl.program_id(2)
    qm = _head_lanes(q_ref[...], hd)
    lane = lax.broadcasted_iota(jnp.int32, (1, 2 * hd), 1)
    rc = min(t, ATT_ROWS)
    row = lax.broadcasted_iota(jnp.int32, (rc, t), 0)
    col = lax.broadcasted_iota(jnp.int32, (rc, t), 1)
    ts = min(t, SB_SUB)
    nsub = t // ts
    suffix = (lax.broadcasted_iota(jnp.int32, (ts, ts), 0)
              > lax.broadcasted_iota(jnp.int32, (ts, ts), 1)).astype(BF16)
    r_scr[...] = jnp.zeros_like(r_scr)
    acc_scr[...] = jnp.zeros_like(acc_scr)

    def block(ki, carry, masked):
        ks = pl.multiple_of(ki * t, t)
        k = k_ref[pl.ds(ks, t), :]
        v = v_ref[pl.ds(ks, t), :]
        for hh in range(2):
            s_scr[hh] = _bdot_nt(qm[hh], k)
        for hh in range(2):
            bias = bsb_ref[2 * pair + hh]
            for c in range(t // rc):
                rs = slice(c * rc, (c + 1) * rc)
                z = s_scr[hh, rs, :] + bias
                sp = _softplus(z)
                if masked:
                    sp = jnp.where(col < row + c * rc, sp, 0.0)
                sp_scr[hh, rs, :] = sp.astype(BF16)
                tail = r_scr[hh, rs, :]
                parts = [None] * nsub
                for u in reversed(range(nsub)):
                    us = slice(u * ts, (u + 1) * ts)
                    parts[u] = z[:, us] - sp[:, us] - tail
                    tail = tail + jnp.sum(sp[:, us], axis=-1, keepdims=True)
                r_scr[hh, rs, :] = tail
                s_scr[hh, rs, :] = parts[0] if nsub == 1 else jnp.concatenate(parts, axis=1)
            for u in range(nsub):
                us = slice(u * ts, (u + 1) * ts)
                lt_scr[hh, :, us] = jnp.dot(sp_scr[hh, :, us], suffix, preferred_element_type=F32)
            for c in range(t // rc):
                rs = slice(c * rc, (c + 1) * rc)
                wgt = jnp.exp(s_scr[hh, rs, :] - lt_scr[hh, rs, :])
                if masked:
                    wgt = jnp.where(col < row + c * rc, wgt, 0.0)
                p_scr[hh, rs, :] = wgt.astype(BF16)
            acc_scr[hh] = acc_scr[hh] + jnp.dot(p_scr[hh], v, preferred_element_type=F32)
        return carry

    block(qi, 0, True)
    lax.fori_loop(0, qi, lambda j, cr: block(qi - 1 - j, cr, False), 0)
    o_ref[...] = jnp.where(lane < hd, acc_scr[0], acc_scr[1]).astype(o_ref.dtype)


def _attn_prompt(fqb, fkb, fvb, fct, sqb, skb, svb, b_sb, nb, seq, nh, t):
    n, w = fqb.shape
    hd = w // nh
    npair = nh // 2
    nq = seq // t
    qspec = pl.BlockSpec((t, 2 * hd), lambda b, p, i, *_: (b * nq + i, p))
    kspec = pl.BlockSpec((seq, 2 * hd), lambda b, p, i, *_: (b, p))
    fo = pl.pallas_call(
        functools.partial(_fox_kernel, t=t, hd=hd),
        grid=(nb, npair, nq),
        in_specs=[qspec, kspec, kspec,
                  pl.BlockSpec((1, 1, 2, seq), lambda b, p, i: (b, p, 0, 0))],
        out_specs=qspec,
        out_shape=jax.ShapeDtypeStruct((n, w), BF16),
        compiler_params=_cparams("parallel", "parallel", "arbitrary"),
        name="fox_prompt",
    )(fqb, fkb, fvb, fct.reshape(nb, npair, 2, seq))
    so = pl.pallas_call(
        functools.partial(_sb_kernel, t=t, hd=hd),
        grid_spec=pltpu.PrefetchScalarGridSpec(
            num_scalar_prefetch=1, grid=(nb, npair, nq),
            in_specs=[qspec, kspec, kspec], out_specs=qspec,
            scratch_shapes=[pltpu.VMEM((2, t, t), F32), pltpu.VMEM((2, t, t), BF16), pltpu.VMEM((2, t, t), F32),
                            pltpu.VMEM((2, t, t), BF16), pltpu.VMEM((2, t, 1), F32),
                            pltpu.VMEM((2, t, 2 * hd), F32)]),
        out_shape=jax.ShapeDtypeStruct((n, w), BF16),
        compiler_params=_cparams("parallel", "parallel", "arbitrary"),
        name="sb_prompt",
    )(b_sb.astype(F32), sqb, skb, svb)
    return fo, so


def _attn_sample_kernel(pt_ref, q_f_ref, kn_f_ref, vn_f_ref, q_s_ref, kn_s_ref, vn_s_ref, frow_ref, bcol_ref,
                        *rest, nh, hd, nt, ps, gp):
    pages = rest[:5 * gp]
    fo_ref, so_ref = rest[5 * gp:5 * gp + 2]
    m_scr, l_scr, af_scr, r_scr, as_scr, c_scr = rest[5 * gp + 2:]
    p = pl.program_id(1)
    rows = nh * nt
    w = nh * hd
    rhead = lax.shift_right_logical(lax.broadcasted_iota(jnp.int32, (rows, w), 0), int(math.log2(nt)))
    lhead = lax.shift_right_logical(lax.broadcasted_iota(jnp.int32, (rows, w), 1), int(math.log2(hd)))
    own = rhead == lhead
    tok = lax.broadcasted_iota(jnp.int32, (rows, ps), 0) & (nt - 1)
    key = lax.broadcasted_iota(jnp.int32, (rows, ps), 1)

    def suffix(n):
        return (lax.broadcasted_iota(jnp.int32, (n, n), 0) > lax.broadcasted_iota(jnp.int32, (n, n), 1)).astype(BF16)

    def qbd(q_ref):
        q = q_ref[...]
        return jnp.where(own, jnp.concatenate([q] * nh, axis=0), 0.0).astype(BF16)

    def scores(q, kts):
        return jnp.concatenate([_bdot(q, kt) for kt in kts], axis=1)

    def weighted(pr, vts):
        out = _bdot_nt(pr[:, 0:ps], vts[0])
        for j in range(1, len(vts)):
            out = out + _bdot_nt(pr[:, j * ps:(j + 1) * ps], vts[j])
        return out

    def fox_update(kts, vts, bias, mask):
        s = scores(qbd(q_f_ref), kts) + bias
        if mask is not None:
            s = jnp.where(mask, s, -jnp.inf)
        m = m_scr[...]
        mn = jnp.maximum(m, jnp.max(s, axis=-1, keepdims=True))
        alpha = jnp.exp(m - mn)
        pr = jnp.exp(s - mn)
        l_scr[...] = alpha * l_scr[...] + jnp.sum(pr, axis=-1, keepdims=True)
        af_scr[...] = alpha * af_scr[...] + weighted(pr, vts)
        m_scr[...] = mn

    def sb_update(kts, vts, mask):
        z = scores(qbd(q_s_ref), kts) + bcol_ref[...]
        sp = _softplus(z)
        if mask is not None:
            sp = jnp.where(mask, sp, 0.0)
        later = jnp.dot(sp.astype(BF16), suffix(z.shape[1]), preferred_element_type=F32)
        wgt = jnp.exp(z - sp - later - r_scr[...])
        if mask is not None:
            wgt = jnp.where(mask, wgt, 0.0)
        r_scr[...] = r_scr[...] + jnp.sum(sp, axis=-1, keepdims=True)
        as_scr[...] = as_scr[...] + weighted(wgt, vts)

    @pl.when(p == 0)
    def _():
        m_scr[...] = jnp.full_like(m_scr, -jnp.inf)
        l_scr[...] = jnp.zeros_like(l_scr)
        af_scr[...] = jnp.zeros_like(af_scr)
        r_scr[...] = jnp.zeros_like(r_scr)
        as_scr[...] = jnp.zeros_like(as_scr)
        c_scr[...] = jnp.zeros_like(c_scr)
        fox_update([kn_f_ref[0]], [vn_f_ref[0]], -frow_ref[0], key <= tok)
        sb_update([kn_s_ref[0]], [vn_s_ref[0]], key < tok)

    @pl.when(p > 0)
    def _():
        lp = jnp.concatenate([pages[5 * j + 2][0] for j in range(gp)], axis=1)
        sfx = suffix(gp * ps)
        hi, mid, lo = _split3(lp)
        suf = (jnp.dot(hi, sfx, preferred_element_type=F32) + jnp.dot(mid, sfx, preferred_element_type=F32)
               + jnp.dot(lo, sfx, preferred_element_type=F32)) + c_scr[...]
        c_scr[...] = c_scr[...] + jnp.sum(lp, axis=-1, keepdims=True)
        fexp = jnp.concatenate([jnp.broadcast_to(suf[h:h + 1], (nt, gp * ps)) for h in range(nh)], axis=0)
        fox_update([pages[5 * j][0] for j in range(gp)], [pages[5 * j + 1][0] for j in range(gp)], fexp, None)
        sb_update([pages[5 * j + 3][0] for j in range(gp)], [pages[5 * j + 4][0] for j in range(gp)], None)

    @pl.when(p == pl.num_programs(1) - 1)
    def _():
        def extract(acc):
            out = jnp.zeros((nt, w), F32)
            for h in range(nh):
                out = out + jnp.where(own[h * nt:(h + 1) * nt], acc[h * nt:(h + 1) * nt], 0.0)
            return out
        fo_ref[...] = extract(af_scr[...] / l_scr[...]).astype(fo_ref.dtype)
        so_ref[...] = extract(as_scr[...]).astype(so_ref.dtype)


def _attn_sample(fqt, fkt, fvt, fct, sqt, skt, svt, b_sb, pools, page_table, nb, nt, nh):
    pool_fk, pool_fv, pool_lf, pool_sk, pool_sv = pools
    w, n = fqt.shape
    hd = w // nh
    nphys, ps = pool_fk.shape[:2]
    npg = page_table.shape[1]
    rows = nh * nt
    gp = next(g for g in (4, 2, 1) if npg % g == 0)
    ngrp = npg // gp
    assert (nt & (nt - 1)) == 0 and (hd & (hd - 1)) == 0 and nt <= ps

    def newpage(at):
        return jnp.pad(at.astype(F32).reshape(w, nb, nt).transpose(1, 0, 2), ((0, 0), (0, 0), (0, ps - nt)))

    fnew = fct.reshape(nh, nb, nt).transpose(1, 0, 2)
    frow = jnp.pad(jnp.repeat(fnew, nt, axis=1), ((0, 0), (0, 0), (0, ps - nt)))
    bcol = jnp.repeat(b_sb.astype(F32), nt).reshape(rows, 1)
    native = lambda a: a.transpose(0, 2, 3, 1).reshape(nphys, w, ps)
    lpt = pool_lf.astype(F32).transpose(0, 2, 1)
    pt = page_table.reshape(-1).astype(jnp.int32)

    qspec = pl.BlockSpec((nt, w), lambda b, p, pt: (b, 0))
    nspec = pl.BlockSpec((1, w, ps), lambda b, p, pt: (b, 0, 0))

    def page(j):
        return lambda b, p, pt: (pt[b * npg + (ngrp - jnp.maximum(p, 1)) * gp + j], 0, 0)

    page_specs, page_args = [], []
    for j in range(gp):
        page_specs += [pl.BlockSpec((1, w, ps), page(j)), pl.BlockSpec((1, w, ps), page(j)),
                       pl.BlockSpec((1, nh, ps), page(j)),
                       pl.BlockSpec((1, w, ps), page(j)), pl.BlockSpec((1, w, ps), page(j))]
        page_args += [native(pool_fk), native(pool_fv), lpt, native(pool_sk), native(pool_sv)]
    return pl.pallas_call(
        functools.partial(_attn_sample_kernel, nh=nh, hd=hd, nt=nt, ps=ps, gp=gp),
        grid_spec=pltpu.PrefetchScalarGridSpec(
            num_scalar_prefetch=1, grid=(nb, ngrp + 1),
            in_specs=[qspec, nspec, nspec, qspec, nspec, nspec,
                      pl.BlockSpec((1, rows, ps), lambda b, p, pt: (b, 0, 0)),
                      pl.BlockSpec((rows, 1), lambda b, p, pt: (0, 0))] + page_specs,
            out_specs=[qspec, qspec],
            scratch_shapes=[pltpu.VMEM((rows, 1), F32), pltpu.VMEM((rows, 1), F32), pltpu.VMEM((rows, w), F32),
                            pltpu.VMEM((rows, 1), F32), pltpu.VMEM((rows, w), F32), pltpu.VMEM((nh, 1), F32)]),
        out_shape=[jax.ShapeDtypeStruct((n, w), F32), jax.ShapeDtypeStruct((n, w), F32)],
        compiler_params=_cparams("parallel", "arbitrary"),
        name="attn_sample",
    )(pt, fqt.astype(F32).T, newpage(fkt), newpage(fvt), sqt.astype(F32).T, newpage(skt), newpage(svt),
      frow, bcol, *page_args)


def _trunk(x3, mod, ab_state, past, page_table, wts, t_attn):
    nb, seq, d = x3.shape
    n = nb * seq
    x = x3.reshape(n, d)
    tm = _tile(n, 256)
    if seq % tm == 0:
        modg = [mod[l][:, None, :] for l in range(2)]
        tpb = seq // tm
    else:
        assert tm % seq == 0
        modg = [jnp.repeat(mod[l], seq, axis=0).reshape(n // tm, tm, -1) for l in range(2)]
        tpb = 1
    nh = wts['nh_att']

    x = _ffn(x, modg[0], 0, tpb, tm, wts['ffn_in'][0][0], wts['ffn_out'][0][0])
    proj = _proj(x, modg[0], 3, tpb, tm, wts['ab_w_in'])
    w5 = wts['s5_glu_w'].shape[0]
    ys, s5_re, s5_im = _s5(proj[:, :w5], ab_state[0], ab_state[1], wts['s5'], nb, seq)
    oh, hg = _hgrn(proj, w5, wts['hgrn_lb'], wts['hgrn_norm_w'], ab_state[2], nb, seq)
    x = _ab_out(x, modg[0], 5, tpb, tm, ys, oh, wts['s5_glu_w'], wts['s5_glu_b'], wts['ab_w_out'])
    x = _ffn(x, modg[0], 6, tpb, tm, wts['ffn_in'][0][1], wts['ffn_out'][0][1])

    x = _ffn(x, modg[1], 0, tpb, tm, wts['ffn_in'][1][0], wts['ffn_out'][1][0])
    (fkt, fvt, skt, svt, fqb, fkb, fvb, sqb, skb, svb, lft, fct) = _cd_in(
        x, modg[1], 3, tpb, tm, wts['cd_wt'], wts['cd_wk'], wts['cd_wf'], wts['cd_bf'], nb, seq)
    if past is None:
        fo, so = _attn_prompt(fqb, fkb, fvb, fct, sqb, skb, svb, wts['cd_b_sb'], nb, seq, nh, t_attn)
    else:
        fo, so = _attn_sample(fqb.T, fkt[0], fvt[0], fct[0], sqb.T, skt[0], svt[0], wts['cd_b_sb'],
                              past, page_table, nb, seq, nh)
    x = _cd_out(x, modg[1], 5, tpb, tm, fo, so, wts['cd_w_out'])
    y = _ffn(x, modg[1], 6, tpb, tm, wts['ffn_in'][1][1], wts['ffn_out'][1][1], wts['final_norm_w'])

    hd = fkt.shape[1] // nh
    if fkt.shape[0] == nb:
        kv = lambda a: a.reshape(nb, nh, hd, seq).transpose(0, 3, 1, 2)
        logf = lft.transpose(0, 2, 1)
    else:
        kv = lambda a: a[0].T.reshape(nb, seq, nh, hd)
        logf = lft[0].T.reshape(nb, seq, nh)
    return (y.reshape(nb, seq, d),
            (s5_re, s5_im, hg, kv(fkt), kv(fvt), logf, kv(skt), kv(svt)))


def kernel(x_prompt, x_sample, state_s5_re, state_s5_im, state_hgrn, cache_fox_k, cache_fox_v,
           cache_fox_logf, cache_sb_k, cache_sb_v, page_table, c_prompt, c_sample, ada_w, ada_b,
           ffn_w_in, ffn_w_out, ab_w_in, ab_w_out, s5_a_re, s5_a_im, s5_log_dt, s5_b_re, s5_b_im,
           s5_c_re, s5_c_im, s5_d, s5_glu_w, s5_glu_b, hgrn_lb_logits, hgrn_norm_w, cd_w_in, cd_b_f,
           cd_b_sb, cd_w_out, final_norm_w):
    bp, seq_p, d = x_prompt.shape
    bs = x_sample.shape[0]
    nh_att, hd = cache_fox_k.shape[2:]
    wa = nh_att * hd
    g, p = s5_a_re.shape
    nhh, dk, dv = state_hgrn.shape[1:]

    nc_all = bp + bs
    c_all = jnp.pad(jnp.concatenate([c_prompt, c_sample], axis=0), ((0, (-nc_all) % 8), (0, 0)))
    mod = _ada(c_all, ada_w, ada_b)
    mod_p = [mod[l, :bp] for l in range(2)]
    mod_s = [mod[l, bp:nc_all] for l in range(2)]

    lb = jnp.cumsum(jax.nn.softmax(hgrn_lb_logits.astype(F32), axis=0), axis=0)[0].reshape(1, nhh * dk)

    wts = dict(
        nh_att=nh_att,
        ffn_in=[[ffn_w_in[l, s].astype(BF16) for s in range(2)] for l in range(2)],
        ffn_out=[[ffn_w_out[l, s].astype(BF16) for s in range(2)] for l in range(2)],
        ab_w_in=ab_w_in.astype(BF16), ab_w_out=ab_w_out.astype(BF16),
        s5=(s5_a_re, s5_a_im, s5_log_dt, s5_b_re, s5_b_im, s5_c_re, s5_c_im, s5_d),
        s5_glu_w=s5_glu_w.astype(BF16), s5_glu_b=s5_glu_b, hgrn_lb=lb, hgrn_norm_w=hgrn_norm_w,
        cd_wt=jnp.concatenate([cd_w_in[:, wa:3 * wa], cd_w_in[:, 4 * wa + nh_att:]], axis=1).T.astype(BF16),
        cd_wk=jnp.concatenate([cd_w_in[:, :3 * wa], cd_w_in[:, 3 * wa + nh_att:]], axis=1).astype(BF16),
        cd_wf=cd_w_in[:, 3 * wa:3 * wa + nh_att].T.astype(BF16),
        cd_bf=cd_b_f.astype(F32).reshape(nh_att, 1),
        cd_b_sb=cd_b_sb, cd_w_out=cd_w_out.astype(BF16), final_norm_w=final_norm_w)

    zeros = lambda *s: jnp.zeros(s, F32)
    y_p, st_p = _trunk(x_prompt, mod_p, (zeros(bp, g, p), zeros(bp, g, p), zeros(bp, nhh, dk, dv)),
                       None, None, wts, _tile(seq_p, ATT_TILE))
    y_s, st_s = _trunk(x_sample, mod_s, (state_s5_re, state_s5_im, state_hgrn),
                       (cache_fox_k, cache_fox_v, cache_fox_logf, cache_sb_k, cache_sb_v),
                       page_table, wts, None)
    return (y_p, y_s) + st_p + st_s
```
